```python
import math
import jax
import jax.numpy as jnp
from jax import lax
import numpy as np

D_MODEL = 1024
BATCH = 8
SEQ = 2048
DEPTH = 4
DEC_BATCH = 128
DEC_SEQ = 8
PAST_LEN = 8192
PAGE_SIZE = 128

N_A_LAYERS = DEPTH // 2
N_B_LAYERS = DEPTH - N_A_LAYERS
EXPAND = 2
D_INNER = EXPAND * D_MODEL
A_HEAD_K = 128
A_HEADS = D_INNER // A_HEAD_K
A_HEAD_V = D_INNER // A_HEADS
A_CHUNK = 32
B_HEAD_DIM = 64
B_HEADS = D_INNER // B_HEAD_DIM
B_KV_HEADS = B_HEADS // 8
B_GROUP = B_HEADS // B_KV_HEADS
WINDOW = 128
KV_WIDTH = 2 * B_KV_HEADS * B_HEAD_DIM
EPS = 1e-6
F32 = jnp.float32

kernel_name = 'yoco_hgrn2_swa_sink_step'


def rms_norm(x, g):
    xf = x.astype(F32)
    y = xf * lax.rsqrt(jnp.mean(xf * xf, axis=-1, keepdims=True) + EPS)
    return (y * g.astype(F32)).astype(x.dtype)


def hgrn_lower_bounds(p):
    lb = jnp.cumsum(jax.nn.softmax(p.astype(F32), axis=0), axis=0)
    return lb - lb[:1]


def hgrn2_recurrence(q, k, v, logf, s0):
    b, l, h, _ = q.shape
    dv = v.shape[-1]
    c = math.gcd(l, A_CHUNK)
    n = l // c

    def to_chunks(t):
        return t.reshape(b, n, c, h, t.shape[-1]).transpose(1, 0, 3, 2, 4)

    causal = jnp.tril(jnp.ones((c, c), dtype=bool))[None, None, :, :, None]

    def step(s, inp):
        qc, kc, vc, fc = inp
        bc = jnp.cumsum(fc, axis=2)
        o = jnp.einsum('bhtd,bhde->bhte', qc * jnp.exp(bc), s)
        rel = jnp.where(causal, bc[:, :, :, None, :] - bc[:, :, None, :, :], -jnp.inf)
        att = jnp.einsum('bhtsd,bhsd->bhts', qc[:, :, :, None, :] * jnp.exp(rel), kc)
        o = o + jnp.einsum('bhts,bhse->bhte', att, vc)
        b_last = bc[:, :, -1, :]
        s = jnp.exp(b_last)[..., None] * s + jnp.einsum(
            'bhsd,bhse->bhde', kc * jnp.exp(b_last[:, :, None, :] - bc), vc)
        return s, o

    s_fin, o = lax.scan(step, s0, (to_chunks(q), to_chunks(k), to_chunks(v), to_chunks(logf)))
    return o.transpose(1, 0, 3, 2, 4).reshape(b, l, h, dv), s_fin


def hgrn2_layer(x, s0, w_in, w_out, g_norm, g_onorm, lb):
    b, l, _ = x.shape
    q, fz, i, gate = jnp.split(rms_norm(x, g_norm) @ w_in, 4, axis=-1)
    fz = fz.astype(F32)
    lb = lb.astype(F32)
    logf = jnp.logaddexp(jnp.log(lb), jnp.log1p(-lb) + jax.nn.log_sigmoid(fz))
    k = (1.0 - lb) * jax.nn.sigmoid(-fz)
    q = jax.nn.silu(q.astype(F32)) * A_HEAD_K ** -0.5
    heads = lambda t: t.reshape(b, l, A_HEADS, -1)
    o, s_new = hgrn2_recurrence(heads(q), heads(k), heads(i.astype(F32)), heads(logf), s0.astype(F32))
    o = rms_norm(o, g_onorm).reshape(b, l, D_INNER) * jax.nn.silu(gate.astype(F32))
    return x + o.astype(x.dtype) @ w_out, s_new


def shared_kv(x, g_kv, w_kv):
    b, l, _ = x.shape
    return (rms_norm(x, g_kv) @ w_kv).reshape(b, l, 2, B_KV_HEADS, B_HEAD_DIM)


def sink_attend(q, k, v, mask, sinks):
    s = jnp.einsum('...tkgd,...skd->...kgts', q, k).astype(F32) * B_HEAD_DIM ** -0.5
    s = jnp.where(mask[..., None, None, :, :], s, -jnp.inf)
    sink = jnp.broadcast_to(sinks.astype(F32).reshape(B_KV_HEADS, B_GROUP, 1, 1), s.shape[:-1] + (1,))
    p = jax.nn.softmax(jnp.concatenate([s, sink], axis=-1), axis=-1)[..., :-1]
    return jnp.einsum('...kgts,...skd->...tkgd', p.astype(v.dtype), v)


def swa_prompt(q, kv, sinks):
    b, l = q.shape[:2]
    n = l // WINDOW
    qb = q.reshape(b, n, WINDOW, B_KV_HEADS, B_GROUP, B_HEAD_DIM)
    kvb = kv.reshape(b, n, WINDOW, 2, B_KV_HEADS, B_HEAD_DIM)
    prev = jnp.concatenate([jnp.zeros_like(kvb[:, :1]), kvb[:, :-1]], axis=1)
    kv2 = jnp.concatenate([prev, kvb], axis=2)
    t = jnp.arange(WINDOW)[:, None] + WINDOW
    s = jnp.arange(2 * WINDOW)[None, :]
    band = (t - s >= 0) & (t - s <= WINDOW)
    valid = (jnp.arange(n)[:, None] > 0) | (jnp.arange(2 * WINDOW)[None, :] >= WINDOW)
    mask = band[None] & valid[:, None, :]
    o = sink_attend(qb, kv2[:, :, :, 0], kv2[:, :, :, 1], mask, sinks)
    return o.reshape(b, l, D_INNER)


def swa_sample(q, kv_new, kv_buf, sinks):
    bd, l = q.shape[:2]
    wb = kv_buf.shape[1]
    kv_all = jnp.concatenate([kv_buf.astype(kv_new.dtype), kv_new], axis=1)
    q_pos = PAST_LEN + jnp.arange(l)
    k_pos = PAST_LEN - wb + jnp.arange(wb + l)
    d = q_pos[:, None] - k_pos[None, :]
    mask = (d >= 0) & (d <= WINDOW)
    o = sink_attend(q.reshape(bd, l, B_KV_HEADS, B_GROUP, B_HEAD_DIM), kv_all[:, :, 0], kv_all[:, :, 1], mask, sinks)
    return o.reshape(bd, l, D_INNER)


def swa_in(x, w_in, g_norm):
    b, l, _ = x.shape
    q, gate = jnp.split(rms_norm(x, g_norm) @ w_in, 2, axis=-1)
    return q.reshape(b, l, B_HEADS, B_HEAD_DIM), gate


def swa_out(x, o, gate, w_out):
    return x + (o.astype(F32) * jax.nn.silu(gate.astype(F32))).astype(x.dtype) @ w_out


def setup_inputs(seed: int = 0) -> dict:
    key = jax.random.key(seed)
    ks = jax.random.split(key, 17)
    nrm = lambda k, shape, s=1.0: s * jax.random.normal(k, shape, F32)
    wb = min(WINDOW, PAST_LEN)
    return {
        'x_prompt': nrm(ks[0], (BATCH, SEQ, D_MODEL)),
        'x_sample': nrm(ks[1], (DEC_BATCH, DEC_SEQ, D_MODEL)),
        'state_hgrn': nrm(ks[2], (N_A_LAYERS, DEC_BATCH, A_HEADS, A_HEAD_K, A_HEAD_V), 0.3),
        'cache_kv_window': nrm(ks[3], (DEC_BATCH, wb, 2, B_KV_HEADS, B_HEAD_DIM)),
        'w_in_a': nrm(ks[4], (N_A_LAYERS, D_MODEL, 4 * D_INNER), D_MODEL ** -0.5),
        'w_out_a': nrm(ks[5], (N_A_LAYERS, D_INNER, D_MODEL), D_INNER ** -0.5),
        'norm_a': 1.0 + nrm(ks[6], (N_A_LAYERS, D_MODEL), 0.02),
        'onorm_a': 1.0 + nrm(ks[7], (N_A_LAYERS, A_HEAD_V), 0.02),
        'lower_bounds_a': 1.0 + nrm(ks[8], (N_A_LAYERS, D_INNER), 0.5),
        'norm_kv': 1.0 + nrm(ks[9], (D_MODEL,), 0.02),
        'w_kv': nrm(ks[10], (D_MODEL, KV_WIDTH), D_MODEL ** -0.5),
        'w_in_b': nrm(ks[11], (N_B_LAYERS, D_MODEL, 2 * D_INNER), D_MODEL ** -0.5),
        'w_out_b': nrm(ks[12], (N_B_LAYERS, D_INNER, D_MODEL), D_INNER ** -0.5),
        'norm_b': 1.0 + nrm(ks[13], (N_B_LAYERS, D_MODEL), 0.02),
        'sinks_b': nrm(ks[14], (N_B_LAYERS, B_HEADS)),
        'norm_f': 1.0 + nrm(ks[15], (D_MODEL,), 0.02),
    }


def reference(x_prompt, x_sample, state_hgrn, cache_kv_window, w_in_a, w_out_a, norm_a, onorm_a,
              lower_bounds_a, norm_kv, w_kv, w_in_b, w_out_b, norm_b, sinks_b, norm_f):
    lb = hgrn_lower_bounds(lower_bounds_a)
    hp, hs = x_prompt, x_sample
    s_zero = jnp.zeros((x_prompt.shape[0], A_HEADS, A_HEAD_K, A_HEAD_V), F32)
    sp_list, ss_list = [], []
    kv_p = kv_s = None
    for layer in range(DEPTH):
        if layer < N_A_LAYERS:
            hp, sp = hgrn2_layer(hp, s_zero, w_in_a[layer], w_out_a[layer], norm_a[layer], onorm_a[layer], lb[layer])
            hs, ss = hgrn2_layer(hs, state_hgrn[layer], w_in_a[layer], w_out_a[layer], norm_a[layer], onorm_a[layer], lb[layer])
            sp_list.append(sp)
            ss_list.append(ss)
            if layer == N_A_LAYERS - 1:
                kv_p = shared_kv(hp, norm_kv, w_kv)
                kv_s = shared_kv(hs, norm_kv, w_kv)
        else:
            j = layer - N_A_LAYERS
            q, gate = swa_in(hp, w_in_b[j], norm_b[j])
            hp = swa_out(hp, swa_prompt(q, kv_p, sinks_b[j]), gate, w_out_b[j])
            q, gate = swa_in(hs, w_in_b[j], norm_b[j])
            hs = swa_out(hs, swa_sample(q, kv_s, cache_kv_window, sinks_b[j]), gate, w_out_b[j])
    y_prompt = rms_norm(hp, norm_f)
    y_sample = rms_norm(hs, norm_f)
    state_hgrn_prompt = jnp.stack(sp_list).astype(x_prompt.dtype)
    state_hgrn_sample = jnp.stack(ss_list).astype(state_hgrn.dtype)
    kv_window_prompt = kv_p[:, -min(WINDOW, kv_p.shape[1]):]
    kv_window_sample = jnp.concatenate([cache_kv_window.astype(kv_s.dtype), kv_s], axis=1)[:, -cache_kv_window.shape[1]:]
    return (y_prompt, y_sample, state_hgrn_prompt, state_hgrn_sample, kv_window_prompt, kv_window_sample)
```

```python
import functools

import jax
import jax.numpy as jnp
from jax import lax
from jax.experimental import pallas as pl
from jax.experimental.pallas import tpu as pltpu

F32 = jnp.float32
BF16 = jnp.bfloat16

EPS = 1e-6
HEAD_A = 128
HEAD_B = 64
GROUP_B = 8
WINDOW = 128
LANES = 128
VMEM_LIMIT = 56 * 1024 * 1024


def _dot(a, b):
    return jnp.dot(a, b, preferred_element_type=F32)


def _dot_nt(a, b):
    return lax.dot_general(a, b, (((1,), (1,)), ((), ())), preferred_element_type=F32)


def _dot_tn(a, b):
    return lax.dot_general(a, b, (((0,), (0,)), ((), ())), preferred_element_type=F32)


def _silu(x):
    return x * (0.5 * jnp.tanh(0.5 * x) + 0.5)


def _rms(x, g):
    return x * lax.rsqrt(jnp.mean(x * x, axis=-1, keepdims=True) + EPS) * g


_EPILOGUES = {
    "none": lambda y: y,
    "silu": _silu,
    "silu_a": lambda y: _silu(y) * HEAD_A ** -0.5,
    "scale_b": lambda y: y * HEAD_B ** -0.5,
}


def _norm_proj_kernel(x_ref, g_ref, w_ref, *out_refs, widths, epilogues):
    h = _rms(x_ref[...], g_ref[...]).astype(BF16)
    off = 0
    for o_ref, wd, ep in zip(out_refs, widths, epilogues):
        y = _dot(h, w_ref[:, off:off + wd])
        o_ref[...] = _EPILOGUES[ep](y).astype(o_ref.dtype)
        off += wd


def _norm_proj(x2d, g, w, widths, epilogues, dtypes, tm):
    m, d = x2d.shape
    n = w.shape[1]
    tm = min(tm, m)
    assert sum(widths) == n and m % tm == 0
    return pl.pallas_call(
        functools.partial(_norm_proj_kernel, widths=tuple(widths), epilogues=tuple(epilogues)),
        grid=(m // tm,),
        in_specs=[
            pl.BlockSpec((tm, d), lambda i: (i, 0)),
            pl.BlockSpec((1, d), lambda i: (0, 0)),
            pl.BlockSpec((d, n), lambda i: (0, 0), pipeline_mode=pl.Buffered(1)),
        ],
        out_specs=[pl.BlockSpec((tm, wd), lambda i: (i, 0)) for wd in widths],
        out_shape=[jax.ShapeDtypeStruct((m, wd), dt) for wd, dt in zip(widths, dtypes)],
        compiler_params=pltpu.CompilerParams(
            dimension_semantics=("arbitrary",), vmem_limit_bytes=VMEM_LIMIT),
        name="norm_proj",
    )(x2d, g.reshape(1, d).astype(F32), w)


def _out_proj_kernel(a_ref, m_ref, w_ref, x_ref, g_ref, o_ref, *, final_norm, gated):
    a = a_ref[...]
    if gated:
        a = (a * m_ref[...]).astype(BF16)
    y = x_ref[...] + _dot(a, w_ref[...])
    if final_norm:
        y = _rms(y, g_ref[...])
    o_ref[...] = y


def _out_proj(a2d, w, x2d, g_final, tm, gate=None):
    m, k = a2d.shape
    d = w.shape[1]
    tm = min(tm, m)
    final_norm = g_final is not None
    gated = gate is not None
    if not gated:
        gate = jnp.zeros((8, LANES), F32)
    gate_spec = pl.BlockSpec((tm, k), lambda i: (i, 0)) if gated else pl.BlockSpec((8, LANES), lambda i: (0, 0))
    g = (g_final if final_norm else jnp.ones((d,), F32)).reshape(1, d).astype(F32)
    return pl.pallas_call(
        functools.partial(_out_proj_kernel, final_norm=final_norm, gated=gated),
        grid=(m // tm,),
        in_specs=[
            pl.BlockSpec((tm, k), lambda i: (i, 0)),
            gate_spec,
            pl.BlockSpec((k, d), lambda i: (0, 0), pipeline_mode=pl.Buffered(1)),
            pl.BlockSpec((tm, d), lambda i: (i, 0)),
            pl.BlockSpec((1, d), lambda i: (0, 0)),
        ],
        out_specs=pl.BlockSpec((tm, d), lambda i: (i, 0)),
        out_shape=jax.ShapeDtypeStruct((m, d), F32),
        compiler_params=pltpu.CompilerParams(
            dimension_semantics=("arbitrary",), vmem_limit_bytes=VMEM_LIMIT),
        name="out_proj",
    )(a2d, gate, w, x2d, g)


def _gates(fz, log_lb, log_1mlb, one_m_lb):
    e = jnp.exp(-jnp.abs(fz))
    log_sig = jnp.minimum(fz, 0.0) - jnp.log1p(e)
    b = log_1mlb + log_sig
    logf = jnp.maximum(log_lb, b) + jnp.log1p(jnp.exp(-jnp.abs(log_lb - b)))
    r = 1.0 / (1.0 + e)
    key = one_m_lb * jnp.where(fz >= 0.0, e * r, r)
    return logf, key


def _hier_attention(q, k, logf, lvl, chunk):
    row = lax.broadcasted_iota(jnp.int32, (chunk, HEAD_A), 0)
    att = jnp.where(lvl == 0, _dot_nt(q.astype(BF16), k.astype(BF16)), 0.0)
    cs = logf
    tot = logf
    m, level = 1, 1
    while m < chunk:
        a = (q * jnp.exp(cs)).astype(BF16)
        b = (k * jnp.exp(tot - cs)).astype(BF16)
        att = jnp.where(lvl == level, _dot_nt(a, b), att)
        second = (row & m) != 0
        prev = pltpu.roll(tot, m, 0)
        nxt = pltpu.roll(tot, chunk - m, 0)
        cs = cs + jnp.where(second, prev, 0.0)
        tot = tot + jnp.where(second, prev, nxt)
        m *= 2
        level += 1
    return att, cs, tot


def _hgrn_prompt_kernel(q_ref, fz_ref, v_ref, gs_ref, lbp_ref, gon_ref, lvl_ref,
                        o_ref, s_ref, st_scr, *, chunk, n_chunks):
    tb = pl.program_id(2)

    @pl.when(tb == 0)
    def _():
        st_scr[...] = jnp.zeros_like(st_scr)

    log_lb = lbp_ref[0:1, :]
    log_1mlb = lbp_ref[1:2, :]
    one_m_lb = lbp_ref[2:3, :]
    gon = gon_ref[...]
    lvl = lvl_ref[...]

    def body(c, carry):
        r0 = pl.multiple_of(c * chunk, chunk)
        q = q_ref[0, pl.ds(r0, chunk), :].astype(F32)
        v = v_ref[0, pl.ds(r0, chunk), :]
        logf, k = _gates(fz_ref[0, pl.ds(r0, chunk), :], log_lb, log_1mlb, one_m_lb)
        att, cs, tot = _hier_attention(q, k, logf, lvl, chunk)
        st = st_scr[...]
        qt = (q * jnp.exp(cs)).astype(BF16)
        o = _dot_nt(qt, st.astype(BF16)) + _dot(att.astype(BF16), v)
        kt = (k * jnp.exp(tot - cs)).astype(BF16)
        st_scr[...] = jnp.exp(tot[0:1, :]) * st + _dot_tn(v, kt)
        y = _rms(o, gon) * gs_ref[0, pl.ds(r0, chunk), :].astype(F32)
        o_ref[0, pl.ds(r0, chunk), :] = y.astype(o_ref.dtype)
        return carry

    lax.fori_loop(0, n_chunks, body, 0)

    @pl.when(tb == pl.num_programs(2) - 1)
    def _():
        s_ref[0, 0] = st_scr[...].T


def _level_table(chunk):
    t = lax.broadcasted_iota(jnp.int32, (chunk, chunk), 0)
    s = lax.broadcasted_iota(jnp.int32, (chunk, chunk), 1)
    x = t ^ s
    lvl = jnp.zeros((chunk, chunk), jnp.int32)
    m, level = 1, 1
    while m < chunk:
        lvl = jnp.where((x >= m) & (x < 2 * m), level, lvl)
        m *= 2
        level += 1
    return jnp.where(t >= s, lvl, -1)


def _hgrn_prompt(q, fz, v, gs, lbp, gon, tblk, chunk):
    b, l, di = q.shape
    heads = di // HEAD_A
    col = lambda i, h, t: (i, t, h)
    return pl.pallas_call(
        functools.partial(_hgrn_prompt_kernel, chunk=chunk, n_chunks=tblk // chunk),
        grid=(b, heads, l // tblk),
        in_specs=[
            pl.BlockSpec((1, tblk, HEAD_A), col),
            pl.BlockSpec((1, tblk, HEAD_A), col),
            pl.BlockSpec((1, tblk, HEAD_A), col),
            pl.BlockSpec((1, tblk, HEAD_A), col),
            pl.BlockSpec((3, HEAD_A), lambda i, h, t: (0, h)),
            pl.BlockSpec((1, HEAD_A), lambda i, h, t: (0, 0)),
            pl.BlockSpec((chunk, chunk), lambda i, h, t: (0, 0)),
        ],
        out_specs=[
            pl.BlockSpec((1, tblk, HEAD_A), col),
            pl.BlockSpec((1, 1, HEAD_A, HEAD_A), lambda i, h, t: (i, h, 0, 0)),
        ],
        out_shape=[
            jax.ShapeDtypeStruct((b, l, di), BF16),
            jax.ShapeDtypeStruct((b, heads, HEAD_A, HEAD_A), F32),
        ],
        scratch_shapes=[pltpu.VMEM((HEAD_A, HEAD_A), F32)],
        compiler_params=pltpu.CompilerParams(
            dimension_semantics=("arbitrary", "arbitrary", "arbitrary"),
            vmem_limit_bytes=VMEM_LIMIT),
        name="hgrn_prompt",
    )(q, fz, v, gs, lbp, gon.reshape(1, HEAD_A).astype(F32), _level_table(chunk))


def _hgrn_sample_kernel(q_ref, fz_ref, v_ref, gs_ref, s0_ref, lbp_ref, gon_ref,
                        o_ref, s_ref, *, seqs, steps):
    log_lb = lbp_ref[0:1, :]
    log_1mlb = lbp_ref[1:2, :]
    one_m_lb = lbp_ref[2:3, :]
    gon = gon_ref[...]
    row = lax.broadcasted_iota(jnp.int32, (steps, HEAD_A), 0)
    prow = lax.broadcasted_iota(jnp.int32, (HEAD_A, HEAD_A), 0)

    def body(i, carry):
        q = q_ref[i]
        v = v_ref[i]
        logf, k = _gates(fz_ref[i], log_lb, log_1mlb, one_m_lb)
        bc = logf
        sh = 1
        while sh < steps:
            bc = bc + jnp.where(row >= sh, pltpu.roll(bc, sh, 0), 0.0)
            sh *= 2
        o = jnp.sum(q * k, axis=-1, keepdims=True) * v
        for j in range(1, steps):
            ok = row >= j
            w = jnp.exp(jnp.where(ok, bc - pltpu.roll(bc, j, 0), 0.0))
            p = jnp.where(ok, q * pltpu.roll(k, j, 0) * w, 0.0)
            o = o + jnp.sum(p, axis=-1, keepdims=True) * pltpu.roll(v, j, 0)
        s0 = s0_ref[i, 0]
        o = o + _dot((q * jnp.exp(bc)).astype(BF16), s0.astype(BF16))
        tot = bc[steps - 1:steps, :]
        kt = k * jnp.exp(tot - bc)
        z = jnp.where(prow < steps, jnp.concatenate([kt, jnp.zeros((HEAD_A - steps, HEAD_A), F32)], 0),
                      jnp.where(prow == steps, jnp.exp(tot), 0.0))
        zt = z.T
        vp = jnp.concatenate([v, jnp.zeros((HEAD_A - steps, HEAD_A), F32)], 0)
        s_ref[i, 0] = zt[:, steps:steps + 1] * s0 + _dot(zt.astype(BF16), vp.astype(BF16))
        o_ref[i] = (_rms(o, gon) * gs_ref[i]).astype(o_ref.dtype)
        return carry

    lax.fori_loop(0, seqs, body, 0)


def _hgrn_sample(q, fz, v, gs, s0, lbp, gon, seqs):
    b, steps, di = q.shape
    heads = di // HEAD_A
    col = lambda i, h: (i, 0, h)
    st = lambda i, h: (i, h, 0, 0)
    return pl.pallas_call(
        functools.partial(_hgrn_sample_kernel, seqs=seqs, steps=steps),
        grid=(b // seqs, heads),
        in_specs=[
            pl.BlockSpec((seqs, steps, HEAD_A), col),
            pl.BlockSpec((seqs, steps, HEAD_A), col),
            pl.BlockSpec((seqs, steps, HEAD_A), col),
            pl.BlockSpec((seqs, steps, HEAD_A), col),
            pl.BlockSpec((seqs, 1, HEAD_A, HEAD_A), st),
            pl.BlockSpec((3, HEAD_A), lambda i, h: (0, h)),
            pl.BlockSpec((1, HEAD_A), lambda i, h: (0, 0)),
        ],
        out_specs=[
            pl.BlockSpec((seqs, steps, HEAD_A), col),
            pl.BlockSpec((seqs, 1, HEAD_A, HEAD_A), st),
        ],
        out_shape=[
            jax.ShapeDtypeStruct((b, steps, di), BF16),
            jax.ShapeDtypeStruct(s0.shape, F32),
        ],
        compiler_params=pltpu.CompilerParams(
            dimension_semantics=("arbitrary", "arbitrary"), vmem_limit_bytes=VMEM_LIMIT),
        name="hgrn_sample",
    )(q, fz, v, gs, s0, lbp, gon.reshape(1, HEAD_A).astype(F32))


def _softmax_sink(s, valid, sink):
    s = jnp.where(valid, s, -jnp.inf)
    mx = jnp.maximum(jnp.max(s, axis=-1, keepdims=True), sink)
    p = jnp.exp(s - mx)
    den = jnp.sum(p, axis=-1, keepdims=True) + jnp.exp(sink - mx)
    return p, 1.0 / den


def _swa_prompt_kernel(sink_ref, q_ref, kvc_ref, kvp_ref, gs_ref, o_ref, *, kv_heads):
    nb = pl.program_id(1)
    kv2 = jnp.concatenate([kvp_ref[0], kvc_ref[0]], axis=0).astype(BF16)
    kw = kv_heads * HEAD_B
    lane_k = lax.broadcasted_iota(jnp.int32, (2 * WINDOW, LANES), 1)
    lane_q = lax.broadcasted_iota(jnp.int32, (WINDOW, LANES), 1)
    t_i = lax.broadcasted_iota(jnp.int32, (WINDOW, 2 * WINDOW), 0)
    s_i = lax.broadcasted_iota(jnp.int32, (WINDOW, 2 * WINDOW), 1)
    d = t_i + WINDOW - s_i
    valid = (d >= 0) & (d <= WINDOW) & ((s_i >= WINDOW) | (nb > 0))
    zero = jnp.zeros((), BF16)
    pairs_per_group = GROUP_B // 2
    for j in range(kv_heads // 2):
        kk = kv2[:, j * LANES:(j + 1) * LANES]
        vv = kv2[:, kw + j * LANES:kw + (j + 1) * LANES]
        kk_sw = pltpu.roll(kk.astype(F32), HEAD_B, 1).astype(BF16)
        vv_sw = pltpu.roll(vv.astype(F32), HEAD_B, 1).astype(BF16)
        for par in range(2):
            g = 2 * j + par
            if par == 0:
                k2 = jnp.where(lane_k < HEAD_B, kk, kk_sw)
                v_lo = jnp.where(lane_k < HEAD_B, vv, zero)
                v_hi = jnp.where(lane_k >= HEAD_B, vv_sw, zero)
            else:
                k2 = jnp.where(lane_k >= HEAD_B, kk, kk_sw)
                v_lo = jnp.where(lane_k < HEAD_B, vv_sw, zero)
                v_hi = jnp.where(lane_k >= HEAD_B, vv, zero)
            for pr in range(pairs_per_group):
                pidx = g * pairs_per_group + pr
                qp = q_ref[0, :, pidx * LANES:(pidx + 1) * LANES]
                acc = None
                for half, v2 in ((0, v_lo), (1, v_hi)):
                    qh = jnp.where((lane_q >= HEAD_B) == (half == 1), qp, zero)
                    p, rden = _softmax_sink(_dot_nt(qh, k2), valid, sink_ref[2 * pidx + half])
                    oh = _dot(p.astype(BF16), v2) * rden
                    acc = oh if acc is None else acc + oh
                gs = gs_ref[0, :, pidx * LANES:(pidx + 1) * LANES].astype(F32)
                o_ref[0, :, pidx * LANES:(pidx + 1) * LANES] = (acc * gs).astype(o_ref.dtype)


def _swa_prompt(q, kv, gs, sinks):
    b, l, di = q.shape
    kvw = kv.shape[-1]
    kv_heads = kvw // (2 * HEAD_B)
    return pl.pallas_call(
        functools.partial(_swa_prompt_kernel, kv_heads=kv_heads),
        grid=(b, l // WINDOW),
        in_specs=[
            pl.BlockSpec(memory_space=pltpu.SMEM),
            pl.BlockSpec((1, WINDOW, di), lambda i, n: (i, n, 0)),
            pl.BlockSpec((1, WINDOW, kvw), lambda i, n: (i, n, 0)),
            pl.BlockSpec((1, WINDOW, kvw), lambda i, n: (i, jnp.maximum(n - 1, 0), 0)),
            pl.BlockSpec((1, WINDOW, di), lambda i, n: (i, n, 0)),
        ],
        out_specs=pl.BlockSpec((1, WINDOW, di), lambda i, n: (i, n, 0)),
        out_shape=jax.ShapeDtypeStruct((b, l, di), BF16),
        compiler_params=pltpu.CompilerParams(
            dimension_semantics=("arbitrary", "arbitrary"), vmem_limit_bytes=VMEM_LIMIT),
        name="swa_prompt",
    )(sinks.astype(F32), q, kv, kv, gs)


def _swa_sample_kernel(q_ref, kvn_ref, cache_ref, sk_ref, o_ref, *, seqs, steps, kv_heads):
    kw = kv_heads * HEAD_B
    rows = GROUP_B * steps
    t_i = lax.broadcasted_iota(jnp.int32, (rows, WINDOW), 0) % steps
    c_i = lax.broadcasted_iota(jnp.int32, (rows, WINDOW), 1)
    valid_c = c_i >= t_i
    valid_n = c_i <= t_i
    pad = jnp.zeros((WINDOW - steps, LANES), BF16)

    def body(i, carry):
        for g in range(kv_heads):
            j = g // 2
            q = q_ref[i, g]
            kc = cache_ref[i, :, j * LANES:(j + 1) * LANES].astype(BF16)
            vc = cache_ref[i, :, kw + j * LANES:kw + (j + 1) * LANES].astype(BF16)
            kn = jnp.concatenate([kvn_ref[i, :, j * LANES:(j + 1) * LANES].astype(BF16), pad], 0)
            vn = jnp.concatenate([kvn_ref[i, :, kw + j * LANES:kw + (j + 1) * LANES].astype(BF16), pad], 0)
            sink = sk_ref[g][:, 0:1]
            sc = jnp.where(valid_c, _dot_nt(q, kc), -jnp.inf)
            sn = jnp.where(valid_n, _dot_nt(q, kn), -jnp.inf)
            mx = jnp.maximum(jnp.maximum(jnp.max(sc, axis=-1, keepdims=True),
                                         jnp.max(sn, axis=-1, keepdims=True)), sink)
            pc = jnp.exp(sc - mx)
            pn = jnp.exp(sn - mx)
            den = (jnp.sum(pc, axis=-1, keepdims=True) + jnp.sum(pn, axis=-1, keepdims=True)
                   + jnp.exp(sink - mx))
            o_ref[i, g] = (_dot(pc.astype(BF16), vc) + _dot(pn.astype(BF16), vn)) / den
        return carry

    lax.fori_loop(0, seqs, body, 0)


def _swa_sample(q, kv_new, cache, sinks, seqs):
    b, steps, di = q.shape
    kvw = kv_new.shape[-1]
    kv_heads = kvw // (2 * HEAD_B)
    rows = GROUP_B * steps
    qg = q.reshape(b, steps, kv_heads, GROUP_B, HEAD_B).transpose(0, 2, 3, 1, 4).reshape(b, kv_heads, rows, HEAD_B)
    zeros = jnp.zeros_like(qg)
    odd = (jnp.arange(kv_heads) % 2 == 1)[None, :, None, None]
    qpad = jnp.concatenate([jnp.where(odd, zeros, qg), jnp.where(odd, qg, zeros)], axis=-1).astype(BF16)
    sk = jnp.broadcast_to(sinks.astype(F32).reshape(kv_heads, GROUP_B, 1, 1),
                          (kv_heads, GROUP_B, steps, LANES)).reshape(kv_heads, rows, LANES)
    out = pl.pallas_call(
        functools.partial(_swa_sample_kernel, seqs=seqs, steps=steps, kv_heads=kv_heads),
        grid=(b // seqs,),
        in_specs=[
            pl.BlockSpec((seqs, kv_heads, rows, LANES), lambda i: (i, 0, 0, 0)),
            pl.BlockSpec((seqs, steps, kvw), lambda i: (i, 0, 0)),
            pl.BlockSpec((seqs, WINDOW, kvw), lambda i: (i, 0, 0)),
            pl.BlockSpec((kv_heads, rows, LANES), lambda i: (0, 0, 0)),
        ],
        out_specs=pl.BlockSpec((seqs, kv_heads, rows, LANES), lambda i: (i, 0, 0, 0)),
        out_shape=jax.ShapeDtypeStruct((b, kv_heads, rows, LANES), F32),
        compiler_params=pltpu.CompilerParams(
            dimension_semantics=("arbitrary",), vmem_limit_bytes=VMEM_LIMIT),
        name="swa_sample",
    )(qpad, kv_new, cache, sk)
    og = jnp.where(odd, out[..., HEAD_B:], out[..., :HEAD_B])
    return og.reshape(b, kv_heads, GROUP_B, steps, HEAD_B).transpose(0, 3, 1, 2, 4).reshape(b, steps, di)


def _lower_bound_params(p):
    lb = jnp.cumsum(jax.nn.softmax(p.astype(F32), axis=0), axis=0)
    lb = lb - lb[:1]
    return jnp.stack([jnp.log(lb), jnp.log1p(-lb), 1.0 - lb], axis=1)


def kernel(x_prompt, x_sample, state_hgrn, cache_kv_window, w_in_a, w_out_a, norm_a, onorm_a,
           lower_bounds_a, norm_kv, w_kv, w_in_b, w_out_b, norm_b, sinks_b, norm_f):
    bp, lp, dm = x_prompt.shape
    bs, ls, _ = x_sample.shape
    n_a = w_in_a.shape[0]
    n_b = w_in_b.shape[0]
    di = w_out_a.shape[1]
    kvw = w_kv.shape[1]
    kv_heads = kvw // (2 * HEAD_B)
    wb = cache_kv_window.shape[1]
    assert wb == WINDOW and lp % WINDOW == 0 and ls <= WINDOW

    lbp = _lower_bound_params(lower_bounds_a)
    hp = x_prompt.reshape(bp * lp, dm)
    hs = x_sample.reshape(bs * ls, dm)
    cache = cache_kv_window.reshape(bs, wb, kvw)

    sp_list, ss_list = [], []
    for layer in range(n_a):
        w_in = w_in_a[layer].astype(BF16)
        w_out = w_out_a[layer].astype(BF16)
        splits = dict(widths=(di,) * 4, epilogues=("silu_a", "none", "none", "silu"))
        q, fz, v, gs = _norm_proj(hp, norm_a[layer], w_in, dtypes=(BF16, F32, BF16, BF16), tm=256, **splits)
        shp = (bp, lp, di)
        og, sp = _hgrn_prompt(q.reshape(shp), fz.reshape(shp), v.reshape(shp), gs.reshape(shp),
                              lbp[layer], onorm_a[layer], tblk=min(512, lp), chunk=128)
        hp = _out_proj(og.reshape(bp * lp, di), w_out, hp, None, tm=512)
        sp_list.append(sp)

        q, fz, v, gs = _norm_proj(hs, norm_a[layer], w_in, dtypes=(F32, F32, F32, F32), tm=256, **splits)
        shp = (bs, ls, di)
        og, ss = _hgrn_sample(q.reshape(shp), fz.reshape(shp), v.reshape(shp), gs.reshape(shp),
                              state_hgrn[layer], lbp[layer], onorm_a[layer], seqs=8)
        hs = _out_proj(og.reshape(bs * ls, di), w_out, hs, None, tm=512)
        ss_list.append(ss)

    w_kv_b = w_kv.astype(BF16)
    (kv_p,) = _norm_proj(hp, norm_kv, w_kv_b, (kvw,), ("none",), (F32,), tm=512)
    (kv_s,) = _norm_proj(hs, norm_kv, w_kv_b, (kvw,), ("none",), (F32,), tm=512)
    kv_p = kv_p.reshape(bp, lp, kvw)
    kv_s = kv_s.reshape(bs, ls, kvw)

    for j in range(n_b):
        w_in = w_in_b[j].astype(BF16)
        w_out = w_out_b[j].astype(BF16)
        g_final = norm_f if j == n_b - 1 else None
        splits = dict(widths=(di, di), epilogues=("scale_b", "silu"))
        q, gs = _norm_proj(hp, norm_b[j], w_in, dtypes=(BF16, BF16), tm=512, **splits)
        o = _swa_prompt(q.reshape(bp, lp, di), kv_p, gs.reshape(bp, lp, di), sinks_b[j])
        hp = _out_proj(o.reshape(bp * lp, di), w_out, hp, g_final, tm=512)

        q, gs = _norm_proj(hs, norm_b[j], w_in, dtypes=(F32, F32), tm=512, **splits)
        o = _swa_sample(q.reshape(bs, ls, di), kv_s, cache, sinks_b[j], seqs=8)
        hs = _out_proj(o.reshape(bs * ls, di), w_out, hs, g_final, tm=512, gate=gs)

    kv_shape = (2, kv_heads, HEAD_B)
    y_prompt = hp.reshape(bp, lp, dm)
    y_sample = hs.reshape(bs, ls, dm)
    state_prompt = jnp.stack(sp_list).astype(x_prompt.dtype)
    state_sample = jnp.stack(ss_list).astype(state_hgrn.dtype)
    kv_window_prompt = kv_p[:, lp - min(WINDOW, lp):].reshape(bp, min(WINDOW, lp), *kv_shape)
    kv_window_sample = jnp.concatenate([cache, kv_s], axis=1)[:, ls:].reshape(bs, wb, *kv_shape)
    return (y_prompt, y_sample, state_prompt, state_sample, kv_window_prompt, kv_window_sample)
```

```python
import functools

import jax
import jax.numpy as jnp
from jax import lax
from jax.experimental import pallas as pl
from jax.experimental.pallas import tpu as pltpu

F32 = jnp.float32
BF16 = jnp.bfloat16

EPS = 1e-6
HEAD_A = 128
HEAD_B = 64
GROUP_B = 8
WINDOW = 128
LANES = 128
VMEM_LIMIT = 56 * 1024 * 1024


def _dot(a, b):
    return jnp.dot(a, b, preferred_element_type=F32)


def _dot_nt(a, b):
    return lax.dot_general(a, b, (((1,), (1,)), ((), ())), preferred_element_type=F32)


def _dot_tn(a, b):
    return lax.dot_general(a, b, (((0,), (0,)), ((), ())), preferred_element_type=F32)


def _silu(x):
    return x * (0.5 * jnp.tanh(0.5 * x) + 0.5)


def _rms(x, g):
    return x * lax.rsqrt(jnp.mean(x * x, axis=-1, keepdims=True) + EPS) * g


_EPILOGUES = {
    "none": lambda y: y,
    "silu": _silu,
    "silu_a": lambda y: _silu(y) * HEAD_A ** -0.5,
    "scale_b": lambda y: y * HEAD_B ** -0.5,
}


def _norm_proj_kernel(x_ref, g_ref, w_ref, *out_refs, widths, epilogues):
    h = _rms(x_ref[...], g_ref[...]).astype(BF16)
    off = 0
    for o_ref, wd, ep in zip(out_refs, widths, epilogues):
        y = _dot(h, w_ref[:, off:off + wd])
        o_ref[...] = _EPILOGUES[ep](y).astype(o_ref.dtype)
        off += wd


def _norm_proj(x2d, g, w, widths, epilogues, dtypes, tm):
    m, d = x2d.shape
    n = w.shape[1]
    tm = min(tm, m)
    assert sum(widths) == n and m % tm == 0
    return pl.pallas_call(
        functools.partial(_norm_proj_kernel, widths=tuple(widths), epilogues=tuple(epilogues)),
        grid=(m // tm,),
        in_specs=[
            pl.BlockSpec((tm, d), lambda i: (i, 0)),
            pl.BlockSpec((1, d), lambda i: (0, 0)),
            pl.BlockSpec((d, n), lambda i: (0, 0), pipeline_mode=pl.Buffered(1)),
        ],
        out_specs=[pl.BlockSpec((tm, wd), lambda i: (i, 0)) for wd in widths],
        out_shape=[jax.ShapeDtypeStruct((m, wd), dt) for wd, dt in zip(widths, dtypes)],
        compiler_params=pltpu.CompilerParams(
            dimension_semantics=("arbitrary",), vmem_limit_bytes=VMEM_LIMIT),
        name="norm_proj",
    )(x2d, g.reshape(1, d).astype(F32), w)


def _out_proj_kernel(a_ref, m_ref, w_ref, x_ref, g_ref, o_ref, *, final_norm, gated):
    a = a_ref[...]
    if gated:
        a = (a * m_ref[...]).astype(BF16)
    y = x_ref[...] + _dot(a, w_ref[...])
    if final_norm:
        y = _rms(y, g_ref[...])
    o_ref[...] = y


def _out_proj(a2d, w, x2d, g_final, tm, gate=None):
    m, k = a2d.shape
    d = w.shape[1]
    tm = min(tm, m)
    final_norm = g_final is not None
    gated = gate is not None
    if not gated:
        gate = jnp.zeros((8, LANES), F32)
    gate_spec = pl.BlockSpec((tm, k), lambda i: (i, 0)) if gated else pl.BlockSpec((8, LANES), lambda i: (0, 0))
    g = (g_final if final_norm else jnp.ones((d,), F32)).reshape(1, d).astype(F32)
    return pl.pallas_call(
        functools.partial(_out_proj_kernel, final_norm=final_norm, gated=gated),
        grid=(m // tm,),
        in_specs=[
            pl.BlockSpec((tm, k), lambda i: (i, 0)),
            gate_spec,
            pl.BlockSpec((k, d), lambda i: (0, 0), pipeline_mode=pl.Buffered(1)),
            pl.BlockSpec((tm, d), lambda i: (i, 0)),
            pl.BlockSpec((1, d), lambda i: (0, 0)),
        ],
        out_specs=pl.BlockSpec((tm, d), lambda i: (i, 0)),
        out_shape=jax.ShapeDtypeStruct((m, d), F32),
        compiler_params=pltpu.CompilerParams(
            dimension_semantics=("arbitrary",), vmem_limit_bytes=VMEM_LIMIT),
        name="out_proj",
    )(a2d, gate, w, x2d, g)


def _gates(fz, lb, one_m_lb):
    e = jnp.exp(-jnp.abs(fz))
    r = 1.0 / (1.0 + e)
    er = e * r
    pos = fz >= 0.0
    f = lb + one_m_lb * jnp.where(pos, r, er)
    key = one_m_lb * jnp.where(pos, er, r)
    return f, key


SUBLANES = 8


def _decay_scores(q, k, f, lvl_ref, chunk):
    nt = chunk // SUBLANES
    tiles = lambda x: [x[SUBLANES * i:SUBLANES * (i + 1)] for i in range(nt)]
    cat = lambda xs: jnp.concatenate(xs, axis=0)
    lvl = lambda i: lvl_ref[SUBLANES * i:SUBLANES * (i + 1), :]
    row = lax.broadcasted_iota(jnp.int32, (SUBLANES, HEAD_A), 0)
    qt, kt = tiles(q), tiles(k)
    et, tt = tiles(f), tiles(f)
    gt = [None] * nt
    k16 = k.astype(BF16)
    prod = _dot_nt(q.astype(BF16), k16)
    att = [jnp.where(lvl(i) == 0, prod[SUBLANES * i:SUBLANES * (i + 1)], 0.0) for i in range(nt)]
    level = 1
    for m in (1, 2, 4):
        a = cat([qt[i] * et[i] for i in range(nt)]).astype(BF16)
        b = k16 if m == 1 else cat([kt[i] * gt[i] for i in range(nt)]).astype(BF16)
        prod = _dot_nt(a, b)
        second = (row & m) != 0
        for i in range(nt):
            att[i] = jnp.where(lvl(i) == level, prod[SUBLANES * i:SUBLANES * (i + 1)], att[i])
            x = jnp.where(second, pltpu.roll(tt[i], m, 0), 1.0)
            y = jnp.where(second, 1.0, pltpu.roll(tt[i], SUBLANES - m, 0))
            et[i] = et[i] * x
            gt[i] = y if gt[i] is None else gt[i] * y
            tt[i] = tt[i] * (x * y)
        level += 1
    mt = 1
    while mt < nt:
        is_second = lambda i: (i // mt) % 2 == 1
        a = cat([qt[i] * et[i] if is_second(i) else qt[i] for i in range(nt)]).astype(BF16)
        b = cat([kt[i] if is_second(i) else kt[i] * gt[i] for i in range(nt)]).astype(BF16)
        prod = _dot_nt(a, b)
        new_t = list(tt)
        for j in range(0, nt, 2 * mt):
            t_first, t_second = tt[j], tt[j + mt]
            both = t_first * t_second
            for i in range(j, j + mt):
                gt[i] = gt[i] * t_second
                new_t[i] = both
            for i in range(j + mt, j + 2 * mt):
                att[i] = jnp.where(lvl(i) == level, prod[SUBLANES * i:SUBLANES * (i + 1)], att[i])
                et[i] = et[i] * t_first
                new_t[i] = both
        tt = new_t
        mt *= 2
        level += 1
    return cat(att), cat(et), cat(gt), tt[0][0:1, :]


def _hgrn_prompt_kernel(q_ref, fz_ref, v_ref, gs_ref, lbp_ref, gon_ref, lvl_ref, *rest,
                        chunk, n_chunks, heads):
    o_ref, s_ref, st_scr, o_scr = rest[-4:]
    tb = pl.program_id(2)

    @pl.when(tb == 0)
    def _():
        st_scr[...] = jnp.zeros_like(st_scr)

    gon = gon_ref[...]
    o_scr[...] = jnp.zeros_like(o_scr)

    def finish(c):
        rows = pl.ds(pl.multiple_of(c * chunk, chunk), chunk)
        for h in range(heads):
            cols = slice(h * HEAD_A, (h + 1) * HEAD_A)
            y = _rms(o_scr[h], gon) * gs_ref[0, rows, cols].astype(F32)
            o_ref[0, rows, cols] = y.astype(o_ref.dtype)

    def body(c, carry):
        finish(jnp.maximum(c - 1, 0))
        rows = pl.ds(pl.multiple_of(c * chunk, chunk), chunk)
        for h in range(heads):
            cols = slice(h * HEAD_A, (h + 1) * HEAD_A)
            q = q_ref[0, rows, cols].astype(F32)
            v = v_ref[0, rows, cols]
            f, k = _gates(fz_ref[0, rows, cols], lbp_ref[0:1, cols], lbp_ref[1:2, cols])
            att, e, g, tot = _decay_scores(q, k, f, lvl_ref, chunk)
            st = st_scr[h]
            o_scr[h] = _dot_nt((q * e).astype(BF16), st.astype(BF16)) + _dot(att.astype(BF16), v)
            st_scr[h] = tot * st + _dot_tn(v, (k * g).astype(BF16))
        return carry

    lax.fori_loop(0, n_chunks, body, 0, unroll=2)
    finish(n_chunks - 1)

    @pl.when(tb == pl.num_programs(2) - 1)
    def _():
        for h in range(heads):
            s_ref[0, 0, h] = st_scr[h].T


def _level_table(chunk):
    t = lax.broadcasted_iota(jnp.int32, (chunk, chunk), 0)
    s = lax.broadcasted_iota(jnp.int32, (chunk, chunk), 1)
    x = t ^ s
    lvl = jnp.zeros((chunk, chunk), jnp.int32)
    m, level = 1, 1
    while m < chunk:
        lvl = jnp.where((x >= m) & (x < 2 * m), level, lvl)
        m *= 2
        level += 1
    return jnp.where(t >= s, lvl, -1)


def _hgrn_prompt(q, fz, v, gs, lbp, gon, states, layer, n_layers, tblk, chunk, heads):
    b, l, di = q.shape
    n_heads = di // HEAD_A
    wd = heads * HEAD_A
    col = lambda i, h, t: (i, t, h)
    in_specs = [
        pl.BlockSpec((1, tblk, wd), col),
        pl.BlockSpec((1, tblk, wd), col),
        pl.BlockSpec((1, tblk, wd), col),
        pl.BlockSpec((1, tblk, wd), col),
        pl.BlockSpec((2, wd), lambda i, h, t: (0, h)),
        pl.BlockSpec((1, HEAD_A), lambda i, h, t: (0, 0)),
        pl.BlockSpec((chunk, chunk), lambda i, h, t: (0, 0)),
    ]
    args = [q, fz, v, gs, lbp, gon.reshape(1, HEAD_A).astype(F32), _level_table(chunk)]
    aliases = {}
    if states is not None:
        in_specs.append(pl.BlockSpec(memory_space=pl.ANY))
        args.append(states)
        aliases = {len(args) - 1: 1}
    return pl.pallas_call(
        functools.partial(_hgrn_prompt_kernel, chunk=chunk, n_chunks=tblk // chunk, heads=heads),
        grid=(b, n_heads // heads, l // tblk),
        in_specs=in_specs,
        out_specs=[
            pl.BlockSpec((1, tblk, wd), col),
            pl.BlockSpec((1, 1, heads, HEAD_A, HEAD_A), lambda i, h, t: (layer, i, h, 0, 0)),
        ],
        out_shape=[
            jax.ShapeDtypeStruct((b, l, di), BF16),
            jax.ShapeDtypeStruct((n_layers, b, n_heads, HEAD_A, HEAD_A), F32),
        ],
        scratch_shapes=[pltpu.VMEM((heads, HEAD_A, HEAD_A), F32), pltpu.VMEM((heads, chunk, HEAD_A), F32)],
        input_output_aliases=aliases,
        compiler_params=pltpu.CompilerParams(
            dimension_semantics=("arbitrary", "arbitrary", "arbitrary"),
            vmem_limit_bytes=VMEM_LIMIT),
        name="hgrn_prompt",
    )(*args)


def _hgrn_sample_kernel(q_ref, fz_ref, v_ref, gs_ref, s0_ref, lbp_ref, gon_ref, *rest, seqs, steps):
    o_ref, s_ref = rest[-2:]
    lb = lbp_ref[0:1, :]
    one_m_lb = lbp_ref[1:2, :]
    gon = gon_ref[...]
    row = lax.broadcasted_iota(jnp.int32, (steps, HEAD_A), 0)
    prow = lax.broadcasted_iota(jnp.int32, (HEAD_A, HEAD_A), 0)
    pad = jnp.zeros((HEAD_A - steps, HEAD_A), F32)

    for i in range(seqs):
        q = q_ref[i]
        v = v_ref[i]
        f, k = _gates(fz_ref[i], lb, one_m_lb)
        o = jnp.sum(q * k, axis=-1, keepdims=True) * v
        w = f
        for j in range(1, steps):
            p = jnp.where(row >= j, q * pltpu.roll(k, j, 0) * w, 0.0)
            o = o + jnp.sum(p, axis=-1, keepdims=True) * pltpu.roll(v, j, 0)
            w = w * pltpu.roll(f, j, 0)
        e = f
        g = jnp.where(row < steps - 1, pltpu.roll(f, steps - 1, 0), 1.0)
        sh = 1
        while sh < steps:
            e = e * jnp.where(row >= sh, pltpu.roll(e, sh, 0), 1.0)
            g = g * jnp.where(row < steps - sh, pltpu.roll(g, steps - sh, 0), 1.0)
            sh *= 2
        s0 = s0_ref[0, i, 0]
        o = o + _dot((q * e).astype(BF16), s0.astype(BF16))
        z = jnp.where(prow == steps, e[steps - 1:steps, :], jnp.concatenate([k * g, pad], 0))
        zt = z.T
        vp = jnp.concatenate([v, pad], 0)
        s_ref[0, i, 0] = zt[:, steps:steps + 1] * s0 + _dot(zt.astype(BF16), vp.astype(BF16))
        o_ref[i] = (_rms(o, gon) * gs_ref[i]).astype(o_ref.dtype)


def _hgrn_sample(q, fz, v, gs, state_in, lbp, gon, states, layer, seqs):
    b, steps, di = q.shape
    heads = di // HEAD_A
    col = lambda i, h: (i, 0, h)
    st = lambda i, h: (layer, i, h, 0, 0)
    in_specs = [
        pl.BlockSpec((seqs, steps, HEAD_A), col),
        pl.BlockSpec((seqs, steps, HEAD_A), col),
        pl.BlockSpec((seqs, steps, HEAD_A), col),
        pl.BlockSpec((seqs, steps, HEAD_A), col),
        pl.BlockSpec((1, seqs, 1, HEAD_A, HEAD_A), st),
        pl.BlockSpec((2, HEAD_A), lambda i, h: (0, h)),
        pl.BlockSpec((1, HEAD_A), lambda i, h: (0, 0)),
    ]
    args = [q, fz, v, gs, state_in, lbp, gon.reshape(1, HEAD_A).astype(F32)]
    aliases = {}
    if states is not None:
        in_specs.append(pl.BlockSpec(memory_space=pl.ANY))
        args.append(states)
        aliases = {len(args) - 1: 1}
    return pl.pallas_call(
        functools.partial(_hgrn_sample_kernel, seqs=seqs, steps=steps),
        grid=(b // seqs, heads),
        in_specs=in_specs,
        out_specs=[
            pl.BlockSpec((seqs, steps, HEAD_A), col),
            pl.BlockSpec((1, seqs, 1, HEAD_A, HEAD_A), st),
        ],
        out_shape=[
            jax.ShapeDtypeStruct((b, steps, di), BF16),
            jax.ShapeDtypeStruct(state_in.shape, F32),
        ],
        input_output_aliases=aliases,
        compiler_params=pltpu.CompilerParams(
            dimension_semantics=("arbitrary", "arbitrary"), vmem_limit_bytes=VMEM_LIMIT),
        name="hgrn_sample",
    )(*args)


def _softmax_sink(s, valid, sink):
    s = jnp.where(valid, s, -jnp.inf)
    mx = jnp.maximum(jnp.max(s, axis=-1, keepdims=True), sink)
    p = jnp.exp(s - mx)
    den = jnp.sum(p, axis=-1, keepdims=True) + jnp.exp(sink - mx)
    return p, 1.0 / den


def _swa_prompt_kernel(sink_ref, q_ref, kvc_ref, kvp_ref, gs_ref, o_ref, *, kv_heads):
    nb = pl.program_id(1)
    kv2 = jnp.concatenate([kvp_ref[0], kvc_ref[0]], axis=0).astype(BF16)
    kw = kv_heads * HEAD_B
    lane_k = lax.broadcasted_iota(jnp.int32, (2 * WINDOW, LANES), 1)
    lane_q = lax.broadcasted_iota(jnp.int32, (WINDOW, LANES), 1)
    t_i = lax.broadcasted_iota(jnp.int32, (WINDOW, 2 * WINDOW), 0)
    s_i = lax.broadcasted_iota(jnp.int32, (WINDOW, 2 * WINDOW), 1)
    d = t_i + WINDOW - s_i
    valid = (d >= 0) & (d <= WINDOW) & ((s_i >= WINDOW) | (nb > 0))
    zero = jnp.zeros((), BF16)
    pairs_per_group = GROUP_B // 2
    for j in range(kv_heads // 2):
        kk = kv2[:, j * LANES:(j + 1) * LANES]
        vv = kv2[:, kw + j * LANES:kw + (j + 1) * LANES]
        kk_sw = pltpu.roll(kk.astype(F32), HEAD_B, 1).astype(BF16)
        vv_sw = pltpu.roll(vv.astype(F32), HEAD_B, 1).astype(BF16)
        for par in range(2):
            g = 2 * j + par
            if par == 0:
                k2 = jnp.where(lane_k < HEAD_B, kk, kk_sw)
                v_lo = jnp.where(lane_k < HEAD_B, vv, zero)
                v_hi = jnp.where(lane_k >= HEAD_B, vv_sw, zero)
            else:
                k2 = jnp.where(lane_k >= HEAD_B, kk, kk_sw)
                v_lo = jnp.where(lane_k < HEAD_B, vv_sw, zero)
                v_hi = jnp.where(lane_k >= HEAD_B, vv, zero)
            for pr in range(pairs_per_group):
                pidx = g * pairs_per_group + pr
                qp = q_ref[0, :, pidx * LANES:(pidx + 1) * LANES]
                acc = None
                for half, v2 in ((0, v_lo), (1, v_hi)):
                    qh = jnp.where((lane_q >= HEAD_B) == (half == 1), qp, zero)
                    p, rden = _softmax_sink(_dot_nt(qh, k2), valid, sink_ref[2 * pidx + half])
                    oh = _dot(p.astype(BF16), v2) * rden
                    acc = oh if acc is None else acc + oh
                gs = gs_ref[0, :, pidx * LANES:(pidx + 1) * LANES].astype(F32)
                o_ref[0, :, pidx * LANES:(pidx + 1) * LANES] = (acc * gs).astype(o_ref.dtype)


def _swa_prompt(q, kv, gs, sinks):
    b, l, di = q.shape
    kvw = kv.shape[-1]
    kv_heads = kvw // (2 * HEAD_B)
    return pl.pallas_call(
        functools.partial(_swa_prompt_kernel, kv_heads=kv_heads),
        grid=(b, l // WINDOW),
        in_specs=[
            pl.BlockSpec(memory_space=pltpu.SMEM),
            pl.BlockSpec((1, WINDOW, di), lambda i, n: (i, n, 0)),
            pl.BlockSpec((1, WINDOW, kvw), lambda i, n: (i, n, 0)),
            pl.BlockSpec((1, WINDOW, kvw), lambda i, n: (i, jnp.maximum(n - 1, 0), 0)),
            pl.BlockSpec((1, WINDOW, di), lambda i, n: (i, n, 0)),
        ],
        out_specs=pl.BlockSpec((1, WINDOW, di), lambda i, n: (i, n, 0)),
        out_shape=jax.ShapeDtypeStruct((b, l, di), BF16),
        compiler_params=pltpu.CompilerParams(
            dimension_semantics=("arbitrary", "arbitrary"), vmem_limit_bytes=VMEM_LIMIT),
        name="swa_prompt",
    )(sinks.astype(F32), q, kv, kv, gs)


def _swa_sample_kernel(q_ref, kvn_ref, cache_ref, sk_ref, o_ref, *, seqs, steps, kv_heads):
    kw = kv_heads * HEAD_B
    rows = GROUP_B * steps
    t_i = lax.broadcasted_iota(jnp.int32, (rows, WINDOW), 0) % steps
    c_i = lax.broadcasted_iota(jnp.int32, (rows, WINDOW), 1)
    valid_c = c_i >= t_i
    valid_n = c_i <= t_i
    pad = jnp.zeros((WINDOW - steps, LANES), BF16)
    blocks = [(i, g) for i in range(seqs) for g in range(kv_heads)]
    col = lambda g: slice((g // 2) * LANES, (g // 2 + 1) * LANES)
    vcol = lambda g: slice(kw + (g // 2) * LANES, kw + (g // 2 + 1) * LANES)

    scores = []
    for i, g in blocks:
        q = q_ref[i, g]
        kn = jnp.concatenate([kvn_ref[i, :, col(g)].astype(BF16), pad], 0)
        sc = jnp.where(valid_c, _dot_nt(q, cache_ref[i, :, col(g)].astype(BF16)), -jnp.inf)
        sn = jnp.where(valid_n, _dot_nt(q, kn), -jnp.inf)
        scores.append(jnp.concatenate([sc, sn], axis=1))
    s = jnp.concatenate(scores, axis=0)
    sink = jnp.concatenate([sk_ref[g] for _, g in blocks], axis=0)
    mx = jnp.maximum(jnp.max(s, axis=-1, keepdims=True), sink)
    p = jnp.exp(s - jnp.concatenate([mx, mx], axis=1))
    den = jnp.sum(p, axis=-1, keepdims=True) + jnp.exp(sink - mx)
    p = p.astype(BF16)
    for n, (i, g) in enumerate(blocks):
        r = slice(n * rows, (n + 1) * rows)
        vn = jnp.concatenate([kvn_ref[i, :, vcol(g)].astype(BF16), pad], 0)
        o = _dot(p[r, :WINDOW], cache_ref[i, :, vcol(g)].astype(BF16)) + _dot(p[r, WINDOW:], vn)
        o_ref[i, g] = o / den[r]


def _swa_sample(q, kv_new, cache, sinks, seqs):
    b, steps, di = q.shape
    kvw = kv_new.shape[-1]
    kv_heads = kvw // (2 * HEAD_B)
    rows = GROUP_B * steps
    qg = q.reshape(b, steps, kv_heads, GROUP_B, HEAD_B).transpose(0, 2, 3, 1, 4).reshape(b, kv_heads, rows, HEAD_B)
    zeros = jnp.zeros_like(qg)
    odd = (jnp.arange(kv_heads) % 2 == 1)[None, :, None, None]
    qpad = jnp.concatenate([jnp.where(odd, zeros, qg), jnp.where(odd, qg, zeros)], axis=-1).astype(BF16)
    sk = jnp.broadcast_to(sinks.astype(F32).reshape(kv_heads, GROUP_B, 1, 1),
                          (kv_heads, GROUP_B, steps, LANES)).reshape(kv_heads, rows, LANES)
    out = pl.pallas_call(
        functools.partial(_swa_sample_kernel, seqs=seqs, steps=steps, kv_heads=kv_heads),
        grid=(b // seqs,),
        in_specs=[
            pl.BlockSpec((seqs, kv_heads, rows, LANES), lambda i: (i, 0, 0, 0)),
            pl.BlockSpec((seqs, steps, kvw), lambda i: (i, 0, 0)),
            pl.BlockSpec((seqs, WINDOW, kvw), lambda i: (i, 0, 0)),
            pl.BlockSpec((kv_heads, rows, LANES), lambda i: (0, 0, 0)),
        ],
        out_specs=pl.BlockSpec((seqs, kv_heads, rows, LANES), lambda i: (i, 0, 0, 0)),
        out_shape=jax.ShapeDtypeStruct((b, kv_heads, rows, LANES), F32),
        compiler_params=pltpu.CompilerParams(
            dimension_semantics=("arbitrary",), vmem_limit_bytes=VMEM_LIMIT),
        name="swa_sample",
    )(qpad, kv_new, cache, sk)
    og = jnp.where(odd, out[..., HEAD_B:], out[..., :HEAD_B])
    return og.reshape(b, kv_heads, GROUP_B, steps, HEAD_B).transpose(0, 3, 1, 2, 4).reshape(b, steps, di)


def _lower_bound_params(p):
    lb = jnp.cumsum(jax.nn.softmax(p.astype(F32), axis=0), axis=0)
    lb = lb - lb[:1]
    return jnp.stack([lb, 1.0 - lb], axis=1)


def kernel(x_prompt, x_sample, state_hgrn, cache_kv_window, w_in_a, w_out_a, norm_a, onorm_a,
           lower_bounds_a, norm_kv, w_kv, w_in_b, w_out_b, norm_b, sinks_b, norm_f):
    bp, lp, dm = x_prompt.shape
    bs, ls, _ = x_sample.shape
    n_a = w_in_a.shape[0]
    n_b = w_in_b.shape[0]
    di = w_out_a.shape[1]
    kvw = w_kv.shape[1]
    kv_heads = kvw // (2 * HEAD_B)
    wb = cache_kv_window.shape[1]
    assert wb == WINDOW and lp % WINDOW == 0 and ls <= WINDOW

    lbp = _lower_bound_params(lower_bounds_a)
    hp = x_prompt.reshape(bp * lp, dm)
    hs = x_sample.reshape(bs * ls, dm)
    cache = cache_kv_window.reshape(bs, wb, kvw)

    state_prompt = state_sample = None
    for layer in range(n_a):
        w_in = w_in_a[layer].astype(BF16)
        w_out = w_out_a[layer].astype(BF16)
        splits = dict(widths=(di,) * 4, epilogues=("silu_a", "none", "none", "silu"))
        q, fz, v, gs = _norm_proj(hp, norm_a[layer], w_in, dtypes=(BF16, F32, BF16, BF16), tm=256, **splits)
        shp = (bp, lp, di)
        og, state_prompt = _hgrn_prompt(
            q.reshape(shp), fz.reshape(shp), v.reshape(shp), gs.reshape(shp), lbp[layer], onorm_a[layer],
            state_prompt, layer, n_a, tblk=min(512, lp), chunk=128, heads=4)
        hp = _out_proj(og.reshape(bp * lp, di), w_out, hp, None, tm=512)

        q, fz, v, gs = _norm_proj(hs, norm_a[layer], w_in, dtypes=(F32, F32, F32, F32), tm=256, **splits)
        shp = (bs, ls, di)
        og, state_sample = _hgrn_sample(
            q.reshape(shp), fz.reshape(shp), v.reshape(shp), gs.reshape(shp), state_hgrn, lbp[layer],
            onorm_a[layer], state_sample, layer, seqs=8)
        hs = _out_proj(og.reshape(bs * ls, di), w_out, hs, None, tm=512)

    w_kv_b = w_kv.astype(BF16)
    (kv_p,) = _norm_proj(hp, norm_kv, w_kv_b, (kvw,), ("none",), (F32,), tm=512)
    (kv_s,) = _norm_proj(hs, norm_kv, w_kv_b, (kvw,), ("none",), (F32,), tm=512)
    kv_p = kv_p.reshape(bp, lp, kvw)
    kv_s = kv_s.reshape(bs, ls, kvw)

    for j in range(n_b):
        w_in = w_in_b[j].astype(BF16)
        w_out = w_out_b[j].astype(BF16)
        g_final = norm_f if j == n_b - 1 else None
        splits = dict(widths=(di, di), epilogues=("scale_b", "silu"))
        q, gs = _norm_proj(hp, norm_b[j], w_in, dtypes=(BF16, BF16), tm=512, **splits)
        o = _swa_prompt(q.reshape(bp, lp, di), kv_p, gs.reshape(bp, lp, di), sinks_b[j])
        hp = _out_proj(o.reshape(bp * lp, di), w_out, hp, g_final, tm=512)

        q, gs = _norm_proj(hs, norm_b[j], w_in, dtypes=(F32, F32), tm=512, **splits)
        o = _swa_sample(q.reshape(bs, ls, di), kv_s, cache, sinks_b[j], seqs=8)
        hs = _out_proj(o.reshape(bs * ls, di), w_out, hs, g_final, tm=512, gate=gs)

    kv_shape = (2, kv_heads, HEAD_B)
    y_prompt = hp.reshape(bp, lp, dm)
    y_sample = hs.reshape(bs, ls, dm)
    kv_window_prompt = kv_p[:, lp - min(WINDOW, lp):].reshape(bp, min(WINDOW, lp), *kv_shape)
    kv_window_sample = jnp.concatenate([cache, kv_s], axis=1)[:, ls:].reshape(bs, wb, *kv_shape)
    return (y_prompt, y_sample, state_prompt, state_sample, kv_window_prompt, kv_window_sample)
```

```python
import functools

import jax
import jax.numpy as jnp
from jax import lax
from jax.experimental import pallas as pl
from jax.experimental.pallas import tpu as pltpu

F32 = jnp.float32
BF16 = jnp.bfloat16

EPS = 1e-6
HEAD_A = 128
HEAD_B = 64
GROUP_B = 8
WINDOW = 128
LANES = 128
VMEM_LIMIT = 56 * 1024 * 1024
LOG2E = 1.4426950408889634


def _dot(a, b):
    return jnp.dot(a, b, preferred_element_type=F32)


def _dot_nt(a, b):
    return lax.dot_general(a, b, (((1,), (1,)), ((), ())), preferred_element_type=F32)


def _dot_tn(a, b):
    return lax.dot_general(a, b, (((0,), (0,)), ((), ())), preferred_element_type=F32)


def _silu(x):
    return x * (0.5 * jnp.tanh(0.5 * x) + 0.5)


def _rms(x, g):
    return x * lax.rsqrt(jnp.mean(x * x, axis=-1, keepdims=True) + EPS) * g


_EPILOGUES = {
    "none": lambda y: y,
    "silu": _silu,
    "silu_a": lambda y: _silu(y) * HEAD_A ** -0.5,
    "scale_b": lambda y: y * (HEAD_B ** -0.5 * LOG2E),
}


def _norm_proj_kernel(x_ref, g_ref, w_ref, *out_refs, widths, epilogues):
    h = _rms(x_ref[...], g_ref[...]).astype(BF16)
    off = 0
    for o_ref, wd, ep in zip(out_refs, widths, epilogues):
        y = _dot(h, w_ref[:, off:off + wd])
        o_ref[...] = _EPILOGUES[ep](y).astype(o_ref.dtype)
        off += wd


def _norm_proj(x2d, g, w, widths, epilogues, dtypes, tm):
    m, d = x2d.shape
    n = w.shape[1]
    tm = min(tm, m)
    assert sum(widths) == n and m % tm == 0
    return pl.pallas_call(
        functools.partial(_norm_proj_kernel, widths=tuple(widths), epilogues=tuple(epilogues)),
        grid=(m // tm,),
        in_specs=[
            pl.BlockSpec((tm, d), lambda i: (i, 0)),
            pl.BlockSpec((1, d), lambda i: (0, 0)),
            pl.BlockSpec((d, n), lambda i: (0, 0), pipeline_mode=pl.Buffered(1)),
        ],
        out_specs=[pl.BlockSpec((tm, wd), lambda i: (i, 0)) for wd in widths],
        out_shape=[jax.ShapeDtypeStruct((m, wd), dt) for wd, dt in zip(widths, dtypes)],
        compiler_params=pltpu.CompilerParams(
            dimension_semantics=("arbitrary",), vmem_limit_bytes=VMEM_LIMIT),
        name="norm_proj",
    )(x2d, g.reshape(1, d).astype(F32), w)


def _out_proj_kernel(a_ref, m_ref, w_ref, x_ref, g_ref, o_ref, *, final_norm, gated):
    a = a_ref[...]
    if gated:
        a = (a * m_ref[...]).astype(BF16)
    y = x_ref[...] + _dot(a, w_ref[...])
    if final_norm:
        y = _rms(y, g_ref[...])
    o_ref[...] = y


def _out_proj(a2d, w, x2d, g_final, tm, gate=None):
    m, k = a2d.shape
    d = w.shape[1]
    tm = min(tm, m)
    final_norm = g_final is not None
    gated = gate is not None
    if not gated:
        gate = jnp.zeros((8, LANES), F32)
    gate_spec = pl.BlockSpec((tm, k), lambda i: (i, 0)) if gated else pl.BlockSpec((8, LANES), lambda i: (0, 0))
    g = (g_final if final_norm else jnp.ones((d,), F32)).reshape(1, d).astype(F32)
    return pl.pallas_call(
        functools.partial(_out_proj_kernel, final_norm=final_norm, gated=gated),
        grid=(m // tm,),
        in_specs=[
            pl.BlockSpec((tm, k), lambda i: (i, 0)),
            gate_spec,
            pl.BlockSpec((k, d), lambda i: (0, 0), pipeline_mode=pl.Buffered(1)),
            pl.BlockSpec((tm, d), lambda i: (i, 0)),
            pl.BlockSpec((1, d), lambda i: (0, 0)),
        ],
        out_specs=pl.BlockSpec((tm, d), lambda i: (i, 0)),
        out_shape=jax.ShapeDtypeStruct((m, d), F32),
        compiler_params=pltpu.CompilerParams(
            dimension_semantics=("arbitrary",), vmem_limit_bytes=VMEM_LIMIT),
        name="out_proj",
    )(a2d, gate, w, x2d, g)


def _gates(fz, lb, one_m_lb):
    e = jnp.exp(-jnp.abs(fz))
    r = 1.0 / (1.0 + e)
    er = e * r
    pos = fz >= 0.0
    f = lb + one_m_lb * jnp.where(pos, r, er)
    key = one_m_lb * jnp.where(pos, er, r)
    return f, key


SUBLANES = 8


def _decay_scores(q, k, f, lvl_ref, chunk):
    nt = chunk // SUBLANES
    tiles = lambda x: [x[SUBLANES * i:SUBLANES * (i + 1)] for i in range(nt)]
    cat = lambda xs: jnp.concatenate(xs, axis=0)
    lvl = lambda i: lvl_ref[SUBLANES * i:SUBLANES * (i + 1), :]
    row = lax.broadcasted_iota(jnp.int32, (SUBLANES, HEAD_A), 0)
    qt, kt = tiles(q), tiles(k)
    et, tt = tiles(f), tiles(f)
    gt = [None] * nt
    k16 = k.astype(BF16)
    prod = _dot_nt(q.astype(BF16), k16)
    att = [jnp.where(lvl(i) == 0, prod[SUBLANES * i:SUBLANES * (i + 1)], 0.0) for i in range(nt)]
    level = 1
    for m in (1, 2, 4):
        a = cat([qt[i] * et[i] for i in range(nt)]).astype(BF16)
        b = k16 if m == 1 else cat([kt[i] * gt[i] for i in range(nt)]).astype(BF16)
        prod = _dot_nt(a, b)
        second = (row & m) != 0
        for i in range(nt):
            att[i] = jnp.where(lvl(i) == level, prod[SUBLANES * i:SUBLANES * (i + 1)], att[i])
            x = jnp.where(second, pltpu.roll(tt[i], m, 0), 1.0)
            y = jnp.where(second, 1.0, pltpu.roll(tt[i], SUBLANES - m, 0))
            et[i] = et[i] * x
            gt[i] = y if gt[i] is None else gt[i] * y
            tt[i] = tt[i] * (x * y)
        level += 1
    mt = 1
    while mt < nt:
        is_second = lambda i: (i // mt) % 2 == 1
        a = cat([qt[i] * et[i] if is_second(i) else qt[i] for i in range(nt)]).astype(BF16)
        b = cat([kt[i] if is_second(i) else kt[i] * gt[i] for i in range(nt)]).astype(BF16)
        prod = _dot_nt(a, b)
        new_t = list(tt)
        for j in range(0, nt, 2 * mt):
            t_first, t_second = tt[j], tt[j + mt]
            both = t_first * t_second
            for i in range(j, j + mt):
                gt[i] = gt[i] * t_second
                new_t[i] = both
            for i in range(j + mt, j + 2 * mt):
                att[i] = jnp.where(lvl(i) == level, prod[SUBLANES * i:SUBLANES * (i + 1)], att[i])
                et[i] = et[i] * t_first
                new_t[i] = both
        tt = new_t
        mt *= 2
        level += 1
    return cat(att), cat(et), cat(gt), tt[0][0:1, :]


def _hgrn_layer_kernel(x_ref, gn_ref, win_ref, wout_ref, lbp_ref, gon_ref, lvl_ref, *rest,
                       chunk, n_chunks, heads, groups):
    y_ref, s_ref, st_scr, h_scr = rest[-13:-9]
    bufs = (rest[-9:-5], rest[-5:-1])
    og_scr = rest[-1]
    tb = pl.program_id(1)
    gw = heads * HEAD_A

    @pl.when(tb == 0)
    def _():
        st_scr[...] = jnp.zeros_like(st_scr)

    gon = gon_ref[...]
    h_scr[...] = _rms(x_ref[0], gn_ref[...]).astype(BF16)

    def project(j, buf):
        q_scr, fz_scr, v_scr, gs_scr = buf
        epilogues = (
            lambda y: (_silu(y) * HEAD_A ** -0.5).astype(BF16),
            lambda y: y,
            lambda y: y.astype(BF16),
            lambda y: _silu(y).astype(BF16),
        )

        def piece(n, dst, ep):
            dst[...] = ep(_dot(h_scr[...], win_ref[j, :, n * gw:(n + 1) * gw]))

        return [functools.partial(piece, n, dst, ep) for n, (dst, ep) in enumerate(zip(buf, epilogues))]

    def recur(j, buf):
        q_scr, fz_scr, v_scr, gs_scr = buf

        def piece(c, h):
            rows = slice(c * chunk, (c + 1) * chunk)
            cols = slice(h * HEAD_A, (h + 1) * HEAD_A)
            hd = j * heads + h
            q = q_scr[rows, cols].astype(F32)
            v = v_scr[rows, cols]
            f, k = _gates(fz_scr[rows, cols], lbp_ref[hd, 0:1, :], lbp_ref[hd, 1:2, :])
            att, e, g, tot = _decay_scores(q, k, f, lvl_ref, chunk)
            st = st_scr[hd]
            o = _dot_nt((q * e).astype(BF16), st.astype(BF16)) + _dot(att.astype(BF16), v)
            st_scr[hd] = tot * st + _dot_tn(v, (k * g).astype(BF16))
            y = _rms(o, gon) * gs_scr[rows, cols].astype(F32)
            og_scr[j, rows, cols] = y.astype(BF16)

        return [functools.partial(piece, c, h) for c in range(n_chunks) for h in range(heads)]

    def interleave(a, b):
        for n in range(max(len(a), len(b))):
            if n < len(a):
                a[n]()
            if n < len(b):
                b[n]()

    interleave(project(0, bufs[0]), [])

    def body(jj, carry):
        j = 2 * jj
        interleave(recur(j, bufs[0]), project(j + 1, bufs[1]))
        interleave(recur(j + 1, bufs[1]), project(jnp.minimum(j + 2, groups - 1), bufs[0]))
        return carry

    lax.fori_loop(0, groups // 2, body, 0)
    og = jnp.concatenate([og_scr[j] for j in range(groups)], axis=1)
    y_ref[0] = x_ref[0] + _dot(og, wout_ref[...])

    @pl.when(tb == pl.num_programs(1) - 1)
    def _():
        for hd in range(groups * heads):
            s_ref[0, 0, hd] = st_scr[hd].T


def _level_table(chunk):
    t = lax.broadcasted_iota(jnp.int32, (chunk, chunk), 0)
    s = lax.broadcasted_iota(jnp.int32, (chunk, chunk), 1)
    x = t ^ s
    lvl = jnp.zeros((chunk, chunk), jnp.int32)
    m, level = 1, 1
    while m < chunk:
        lvl = jnp.where((x >= m) & (x < 2 * m), level, lvl)
        m *= 2
        level += 1
    return jnp.where(t >= s, lvl, -1)


def _group_in_weights(w_in, heads):
    d, n = w_in.shape
    gw = heads * HEAD_A
    groups = n // (4 * gw)
    return w_in.astype(BF16).reshape(d, 4, groups, gw).transpose(2, 0, 1, 3).reshape(groups, d, 4 * gw)


def _hgrn_layer_prompt(x, g_norm, w_in_g, w_out, lbp, gon, states, layer, n_layers, tblk, chunk, heads):
    b, l, d = x.shape
    groups = w_in_g.shape[0]
    gw = heads * HEAD_A
    n_heads = groups * heads
    n_chunks = tblk // chunk
    in_specs = [
        pl.BlockSpec((1, tblk, d), lambda i, t: (i, t, 0)),
        pl.BlockSpec((1, d), lambda i, t: (0, 0)),
        pl.BlockSpec((groups, d, 4 * gw), lambda i, t: (0, 0, 0), pipeline_mode=pl.Buffered(1)),
        pl.BlockSpec((n_heads * HEAD_A, d), lambda i, t: (0, 0), pipeline_mode=pl.Buffered(1)),
        pl.BlockSpec((n_heads, 2, HEAD_A), lambda i, t: (0, 0, 0)),
        pl.BlockSpec((1, HEAD_A), lambda i, t: (0, 0)),
        pl.BlockSpec((chunk, chunk), lambda i, t: (0, 0)),
    ]
    args = [x, g_norm.reshape(1, d).astype(F32), w_in_g, w_out,
            lbp.reshape(2, n_heads, HEAD_A).transpose(1, 0, 2), gon.reshape(1, HEAD_A).astype(F32),
            _level_table(chunk)]
    aliases = {}
    if states is not None:
        in_specs.append(pl.BlockSpec(memory_space=pl.ANY))
        args.append(states)
        aliases = {len(args) - 1: 1}
    return pl.pallas_call(
        functools.partial(_hgrn_layer_kernel, chunk=chunk, n_chunks=n_chunks, heads=heads, groups=groups),
        grid=(b, l // tblk),
        in_specs=in_specs,
        out_specs=[
            pl.BlockSpec((1, tblk, d), lambda i, t: (i, t, 0)),
            pl.BlockSpec((1, 1, n_heads, HEAD_A, HEAD_A), lambda i, t: (layer, i, 0, 0, 0)),
        ],
        out_shape=[
            jax.ShapeDtypeStruct((b, l, d), F32),
            jax.ShapeDtypeStruct((n_layers, b, n_heads, HEAD_A, HEAD_A), F32),
        ],
        scratch_shapes=[
            pltpu.VMEM((n_heads, HEAD_A, HEAD_A), F32),
            pltpu.VMEM((tblk, d), BF16),
        ] + 2 * [
            pltpu.VMEM((tblk, gw), BF16),
            pltpu.VMEM((tblk, gw), F32),
            pltpu.VMEM((tblk, gw), BF16),
            pltpu.VMEM((tblk, gw), BF16),
        ] + [
            pltpu.VMEM((groups, tblk, gw), BF16),
        ],
        input_output_aliases=aliases,
        compiler_params=pltpu.CompilerParams(
            dimension_semantics=("arbitrary", "arbitrary"), vmem_limit_bytes=VMEM_LIMIT),
        name="hgrn_layer",
    )(*args)


def _hgrn_sample_kernel(q_ref, fz_ref, v_ref, gs_ref, s0_ref, lbp_ref, gon_ref, *rest, seqs, steps):
    o_ref, s_ref = rest[-2:]
    lb = lbp_ref[0:1, :]
    one_m_lb = lbp_ref[1:2, :]
    gon = gon_ref[...]
    row = lax.broadcasted_iota(jnp.int32, (seqs, steps, HEAD_A), 1)
    prow = lax.broadcasted_iota(jnp.int32, (HEAD_A, HEAD_A), 0)
    pad = jnp.zeros((HEAD_A - steps, HEAD_A), F32)
    roll = lambda x, n: pltpu.roll(x, n, 1)

    q = q_ref[...]
    v = v_ref[...]
    f, k = _gates(fz_ref[...], lb, one_m_lb)
    o = jnp.sum(q * k, axis=-1, keepdims=True) * v
    w = f
    for j in range(1, steps):
        p = jnp.where(row >= j, q * roll(k, j) * w, 0.0)
        o = o + jnp.sum(p, axis=-1, keepdims=True) * roll(v, j)
        w = w * roll(f, j)
    e = f
    g = jnp.where(row < steps - 1, roll(f, steps - 1), 1.0)
    sh = 1
    while sh < steps:
        e = e * jnp.where(row >= sh, roll(e, sh), 1.0)
        g = g * jnp.where(row < steps - sh, roll(g, steps - sh), 1.0)
        sh *= 2
    qe = (q * e).astype(BF16)
    kg = k * g
    outs = []
    for i in range(seqs):
        s0 = s0_ref[0, i, 0]
        outs.append(o[i] + _dot(qe[i], s0.astype(BF16)))
        z = jnp.where(prow == steps, e[i, steps - 1:steps, :], jnp.concatenate([kg[i], pad], 0))
        zt = z.T
        vp = jnp.concatenate([v[i], pad], 0)
        s_ref[0, i, 0] = zt[:, steps:steps + 1] * s0 + _dot(zt.astype(BF16), vp.astype(BF16))
    o_ref[...] = (_rms(jnp.stack(outs), gon) * gs_ref[...]).astype(o_ref.dtype)


def _hgrn_sample(q, fz, v, gs, state_in, lbp, gon, states, layer, seqs):
    b, steps, di = q.shape
    heads = di // HEAD_A
    col = lambda i, h: (i, 0, h)
    st = lambda i, h: (layer, i, h, 0, 0)
    in_specs = [
        pl.BlockSpec((seqs, steps, HEAD_A), col),
        pl.BlockSpec((seqs, steps, HEAD_A), col),
        pl.BlockSpec((seqs, steps, HEAD_A), col),
        pl.BlockSpec((seqs, steps, HEAD_A), col),
        pl.BlockSpec((1, seqs, 1, HEAD_A, HEAD_A), st),
        pl.BlockSpec((2, HEAD_A), lambda i, h: (0, h)),
        pl.BlockSpec((1, HEAD_A), lambda i, h: (0, 0)),
    ]
    args = [q, fz, v, gs, state_in, lbp, gon.reshape(1, HEAD_A).astype(F32)]
    aliases = {}
    if states is not None:
        in_specs.append(pl.BlockSpec(memory_space=pl.ANY))
        args.append(states)
        aliases = {len(args) - 1: 1}
    return pl.pallas_call(
        functools.partial(_hgrn_sample_kernel, seqs=seqs, steps=steps),
        grid=(b // seqs, heads),
        in_specs=in_specs,
        out_specs=[
            pl.BlockSpec((seqs, steps, HEAD_A), col),
            pl.BlockSpec((1, seqs, 1, HEAD_A, HEAD_A), st),
        ],
        out_shape=[
            jax.ShapeDtypeStruct((b, steps, di), BF16),
            jax.ShapeDtypeStruct(state_in.shape, F32),
        ],
        input_output_aliases=aliases,
        compiler_params=pltpu.CompilerParams(
            dimension_semantics=("arbitrary", "arbitrary"), vmem_limit_bytes=VMEM_LIMIT),
        name="hgrn_sample",
    )(*args)


def _softmax_sink(s, valid, sink):
    s = jnp.where(valid, s, -jnp.inf)
    mx = jnp.maximum(jnp.max(s, axis=-1, keepdims=True), sink)
    p = jnp.exp(s - mx)
    den = jnp.sum(p, axis=-1, keepdims=True) + jnp.exp(sink - mx)
    return p, 1.0 / den


def _swa_prompt_kernel(sink_ref, q_ref, kvc_ref, kvp_ref, gs_ref, o_ref, *, kv_heads):
    nb = pl.program_id(1)
    kv2 = jnp.concatenate([kvp_ref[0], kvc_ref[0]], axis=0).astype(BF16)
    kw = kv_heads * HEAD_B
    lane_k = lax.broadcasted_iota(jnp.int32, (2 * WINDOW, LANES), 1)
    lane_q = lax.broadcasted_iota(jnp.int32, (WINDOW, LANES), 1)
    t_i = lax.broadcasted_iota(jnp.int32, (WINDOW, 2 * WINDOW), 0)
    s_i = lax.broadcasted_iota(jnp.int32, (WINDOW, 2 * WINDOW), 1)
    d = t_i + WINDOW - s_i
    valid = (d >= 0) & (d <= WINDOW) & ((s_i >= WINDOW) | (nb > 0))
    zero = jnp.zeros((), BF16)
    pairs_per_group = GROUP_B // 2
    for j in range(kv_heads // 2):
        kk = kv2[:, j * LANES:(j + 1) * LANES]
        vv = kv2[:, kw + j * LANES:kw + (j + 1) * LANES]
        kk_sw = pltpu.roll(kk.astype(F32), HEAD_B, 1).astype(BF16)
        vv_sw = pltpu.roll(vv.astype(F32), HEAD_B, 1).astype(BF16)
        for par in range(2):
            g = 2 * j + par
            if par == 0:
                k2 = jnp.where(lane_k < HEAD_B, kk, kk_sw)
                v_lo = jnp.where(lane_k < HEAD_B, vv, zero)
                v_hi = jnp.where(lane_k >= HEAD_B, vv_sw, zero)
            else:
                k2 = jnp.where(lane_k >= HEAD_B, kk, kk_sw)
                v_lo = jnp.where(lane_k < HEAD_B, vv_sw, zero)
                v_hi = jnp.where(lane_k >= HEAD_B, vv, zero)
            qs, sinks = [], []
            for half in range(2):
                for pr in range(pairs_per_group):
                    pidx = g * pairs_per_group + pr
                    qp = q_ref[0, :, pidx * LANES:(pidx + 1) * LANES]
                    qs.append(jnp.where((lane_q >= HEAD_B) == (half == 1), qp, zero))
                    sinks.append(jnp.full((WINDOW, LANES), sink_ref[2 * pidx + half] * LOG2E, F32))
            sink = jnp.stack(sinks)
            s = _dot_nt(jnp.concatenate(qs, axis=0), k2).reshape(GROUP_B, WINDOW, 2 * WINDOW)
            s = jnp.where(valid[None], s, -jnp.inf)
            mx = jnp.broadcast_to(jnp.max(s, axis=-1, keepdims=True), sink.shape)
            p = jnp.exp2(s - jnp.concatenate([mx, mx], axis=-1))
            den = jnp.sum(p, axis=-1, keepdims=True) + jnp.exp2(sink - mx)
            p = p.astype(BF16).reshape(GROUP_B * WINDOW, 2 * WINDOW)
            rden = (1.0 / den).reshape(2, pairs_per_group * WINDOW, LANES)
            half_rows = pairs_per_group * WINDOW
            o = (_dot(p[:half_rows], v_lo) * rden[0] + _dot(p[half_rows:], v_hi) * rden[1])
            for pr in range(pairs_per_group):
                pidx = g * pairs_per_group + pr
                acc = o[pr * WINDOW:(pr + 1) * WINDOW]
                gs = gs_ref[0, :, pidx * LANES:(pidx + 1) * LANES].astype(F32)
                o_ref[0, :, pidx * LANES:(pidx + 1) * LANES] = (acc * gs).astype(o_ref.dtype)


def _swa_prompt(q, kv, gs, sinks):
    b, l, di = q.shape
    kvw = kv.shape[-1]
    kv_heads = kvw // (2 * HEAD_B)
    return pl.pallas_call(
        functools.partial(_swa_prompt_kernel, kv_heads=kv_heads),
        grid=(b, l // WINDOW),
        in_specs=[
            pl.BlockSpec(memory_space=pltpu.SMEM),
            pl.BlockSpec((1, WINDOW, di), lambda i, n: (i, n, 0)),
            pl.BlockSpec((1, WINDOW, kvw), lambda i, n: (i, n, 0)),
            pl.BlockSpec((1, WINDOW, kvw), lambda i, n: (i, jnp.maximum(n - 1, 0), 0)),
            pl.BlockSpec((1, WINDOW, di), lambda i, n: (i, n, 0)),
        ],
        out_specs=pl.BlockSpec((1, WINDOW, di), lambda i, n: (i, n, 0)),
        out_shape=jax.ShapeDtypeStruct((b, l, di), BF16),
        compiler_params=pltpu.CompilerParams(
            dimension_semantics=("arbitrary", "arbitrary"), vmem_limit_bytes=VMEM_LIMIT),
        name="swa_prompt",
    )(sinks.astype(F32), q, kv, kv, gs)


def _swa_sample_kernel(q_ref, kvn_ref, cache_ref, sk_ref, o_ref, *, seqs, steps, kv_heads):
    kw = kv_heads * HEAD_B
    rows = GROUP_B * steps
    t_i = lax.broadcasted_iota(jnp.int32, (rows, WINDOW), 0) % steps
    c_i = lax.broadcasted_iota(jnp.int32, (rows, WINDOW), 1)
    valid_c = c_i >= t_i
    valid_n = c_i <= t_i
    pad = jnp.zeros((WINDOW - steps, LANES), BF16)
    blocks = [(i, g) for i in range(seqs) for g in range(kv_heads)]
    col = lambda g: slice((g // 2) * LANES, (g // 2 + 1) * LANES)
    vcol = lambda g: slice(kw + (g // 2) * LANES, kw + (g // 2 + 1) * LANES)

    scores = []
    for i, g in blocks:
        q = q_ref[i, g]
        kn = jnp.concatenate([kvn_ref[i, :, col(g)].astype(BF16), pad], 0)
        sc = jnp.where(valid_c, _dot_nt(q, cache_ref[i, :, col(g)].astype(BF16)), -jnp.inf)
        sn = jnp.where(valid_n, _dot_nt(q, kn), -jnp.inf)
        scores.append(jnp.concatenate([sc, sn], axis=1))
    s = jnp.concatenate(scores, axis=0)
    sink = jnp.concatenate([sk_ref[g] for _, g in blocks], axis=0) * LOG2E
    mx = jnp.broadcast_to(jnp.max(s, axis=-1, keepdims=True), sink.shape)
    p = jnp.exp2(s - jnp.concatenate([mx, mx], axis=1))
    den = jnp.sum(p, axis=-1, keepdims=True) + jnp.exp2(sink - mx)
    p = p.astype(BF16)
    for n, (i, g) in enumerate(blocks):
        r = slice(n * rows, (n + 1) * rows)
        vn = jnp.concatenate([kvn_ref[i, :, vcol(g)].astype(BF16), pad], 0)
        o = _dot(p[r, :WINDOW], cache_ref[i, :, vcol(g)].astype(BF16)) + _dot(p[r, WINDOW:], vn)
        o_ref[i, g] = o / den[r]


def _swa_sample(q, kv_new, cache, sinks, seqs):
    b, steps, di = q.shape
    kvw = kv_new.shape[-1]
    kv_heads = kvw // (2 * HEAD_B)
    rows = GROUP_B * steps
    qg = q.reshape(b, steps, kv_heads, GROUP_B, HEAD_B).transpose(0, 2, 3, 1, 4).reshape(b, kv_heads, rows, HEAD_B)
    zeros = jnp.zeros_like(qg)
    odd = (jnp.arange(kv_heads) % 2 == 1)[None, :, None, None]
    qpad = jnp.concatenate([jnp.where(odd, zeros, qg), jnp.where(odd, qg, zeros)], axis=-1).astype(BF16)
    sk = jnp.broadcast_to(sinks.astype(F32).reshape(kv_heads, GROUP_B, 1, 1),
                          (kv_heads, GROUP_B, steps, LANES)).reshape(kv_heads, rows, LANES)
    out = pl.pallas_call(
        functools.partial(_swa_sample_kernel, seqs=seqs, steps=steps, kv_heads=kv_heads),
        grid=(b // seqs,),
        in_specs=[
            pl.BlockSpec((seqs, kv_heads, rows, LANES), lambda i: (i, 0, 0, 0)),
            pl.BlockSpec((seqs, steps, kvw), lambda i: (i, 0, 0)),
            pl.BlockSpec((seqs, WINDOW, kvw), lambda i: (i, 0, 0)),
            pl.BlockSpec((kv_heads, rows, LANES), lambda i: (0, 0, 0)),
        ],
        out_specs=pl.BlockSpec((seqs, kv_heads, rows, LANES), lambda i: (i, 0, 0, 0)),
        out_shape=jax.ShapeDtypeStruct((b, kv_heads, rows, LANES), F32),
        compiler_params=pltpu.CompilerParams(
            dimension_semantics=("arbitrary",), vmem_limit_bytes=VMEM_LIMIT),
        name="swa_sample",
    )(qpad, kv_new, cache, sk)
    og = jnp.where(odd, out[..., HEAD_B:], out[..., :HEAD_B])
    return og.reshape(b, kv_heads, GROUP_B, steps, HEAD_B).transpose(0, 3, 1, 2, 4).reshape(b, steps, di)


def _lower_bound_params(p):
    lb = jnp.cumsum(jax.nn.softmax(p.astype(F32), axis=0), axis=0)
    lb = lb - lb[:1]
    return jnp.stack([lb, 1.0 - lb], axis=1)


def kernel(x_prompt, x_sample, state_hgrn, cache_kv_window, w_in_a, w_out_a, norm_a, onorm_a,
           lower_bounds_a, norm_kv, w_kv, w_in_b, w_out_b, norm_b, sinks_b, norm_f):
    bp, lp, dm = x_prompt.shape
    bs, ls, _ = x_sample.shape
    n_a = w_in_a.shape[0]
    n_b = w_in_b.shape[0]
    di = w_out_a.shape[1]
    kvw = w_kv.shape[1]
    kv_heads = kvw // (2 * HEAD_B)
    wb = cache_kv_window.shape[1]
    assert wb == WINDOW and lp % WINDOW == 0 and ls <= WINDOW

    lbp = _lower_bound_params(lower_bounds_a)
    hp = x_prompt
    hs = x_sample.reshape(bs * ls, dm)
    cache = cache_kv_window.reshape(bs, wb, kvw)

    state_prompt = state_sample = None
    for layer in range(n_a):
        w_in = w_in_a[layer].astype(BF16)
        w_out = w_out_a[layer].astype(BF16)
        splits = dict(widths=(di,) * 4, epilogues=("silu_a", "none", "none", "silu"))
        hp, state_prompt = _hgrn_layer_prompt(
            hp, norm_a[layer], _group_in_weights(w_in_a[layer], 2), w_out, lbp[layer], onorm_a[layer],
            state_prompt, layer, n_a, tblk=min(256, lp), chunk=128, heads=2)

        q, fz, v, gs = _norm_proj(hs, norm_a[layer], w_in, dtypes=(F32, F32, F32, F32), tm=256, **splits)
        shp = (bs, ls, di)
        og, state_sample = _hgrn_sample(
            q.reshape(shp), fz.reshape(shp), v.reshape(shp), gs.reshape(shp), state_hgrn, lbp[layer],
            onorm_a[layer], state_sample, layer, seqs=min(16, bs))
        hs = _out_proj(og.reshape(bs * ls, di), w_out, hs, None, tm=512)

    hp = hp.reshape(bp * lp, dm)
    w_kv_b = w_kv.astype(BF16)
    (kv_p,) = _norm_proj(hp, norm_kv, w_kv_b, (kvw,), ("none",), (F32,), tm=512)
    (kv_s,) = _norm_proj(hs, norm_kv, w_kv_b, (kvw,), ("none",), (F32,), tm=512)
    kv_p = kv_p.reshape(bp, lp, kvw)
    kv_s = kv_s.reshape(bs, ls, kvw)

    for j in range(n_b):
        w_in = w_in_b[j].astype(BF16)
        w_out = w_out_b[j].astype(BF16)
        g_final = norm_f if j == n_b - 1 else None
        splits = dict(widths=(di, di), epilogues=("scale_b", "silu"))
        q, gs = _norm_proj(hp, norm_b[j], w_in, dtypes=(BF16, BF16), tm=512, **splits)
        o = _swa_prompt(q.reshape(bp, lp, di), kv_p, gs.reshape(bp, lp, di), sinks_b[j])
        hp = _out_proj(o.reshape(bp * lp, di), w_out, hp, g_final, tm=512)

        q, gs = _norm_proj(hs, norm_b[j], w_in, dtypes=(F32, F32), tm=512, **splits)
        o = _swa_sample(q.reshape(bs, ls, di), kv_s, cache, sinks_b[j], seqs=8)
        hs = _out_proj(o.reshape(bs * ls, di), w_out, hs, g_final, tm=512, gate=gs)

    kv_shape = (2, kv_heads, HEAD_B)
    y_prompt = hp.reshape(bp, lp, dm)
    y_sample = hs.reshape(bs, ls, dm)
    kv_window_prompt = kv_p[:, lp - min(WINDOW, lp):].reshape(bp, min(WINDOW, lp), *kv_shape)
    kv_window_sample = jnp.concatenate([cache, kv_s], axis=1)[:, ls:].reshape(bs, wb, *kv_shape)
    return (y_prompt, y_sample, state_prompt, state_sample, kv_window_prompt, kv_window_sample)
```

```python
import functools

import jax
import jax.numpy as jnp
from jax import lax
from jax.experimental import pallas as pl
from jax.experimental.pallas import tpu as pltpu

F32 = jnp.float32
BF16 = jnp.bfloat16

EPS = 1e-6
HEAD_A = 128
HEAD_B = 64
GROUP_B = 8
WINDOW = 128
LANES = 128
VMEM_LIMIT = 56 * 1024 * 1024
LOG2E = 1.4426950408889634


def _dot(a, b):
    return jnp.dot(a, b, preferred_element_type=F32)


def _dot_nt(a, b):
    return lax.dot_general(a, b, (((1,), (1,)), ((), ())), preferred_element_type=F32)


def _dot_tn(a, b):
    return lax.dot_general(a, b, (((0,), (0,)), ((), ())), preferred_element_type=F32)


def _silu(x):
    return x * (0.5 * jnp.tanh(0.5 * x) + 0.5)


def _rms(x, g):
    return x * lax.rsqrt(jnp.mean(x * x, axis=-1, keepdims=True) + EPS) * g


def _gates(fz, lb, one_m_lb):
    e = jnp.exp(-jnp.abs(fz))
    r = 1.0 / (1.0 + e)
    er = e * r
    pos = fz >= 0.0
    f = lb + one_m_lb * jnp.where(pos, r, er)
    key = one_m_lb * jnp.where(pos, er, r)
    return f, key


_EPILOGUES = {
    "none": lambda y: y,
    "silu": _silu,
    "silu_a": lambda y: _silu(y) * HEAD_A ** -0.5,
    "scale_b": lambda y: y * (HEAD_B ** -0.5 * LOG2E),
}


def _norm_proj_kernel(x_ref, g_ref, w_ref, lbp_ref, *out_refs, widths, epilogues):
    h = _rms(x_ref[...], g_ref[...]).astype(BF16)
    outs = iter(out_refs)
    off = 0
    for wd, ep in zip(widths, epilogues):
        y = _dot(h, w_ref[:, off:off + wd])
        if ep == "gates":
            f, key = _gates(y, lbp_ref[0:1, :], lbp_ref[1:2, :])
            next(outs)[...] = f
            next(outs)[...] = key
        else:
            o_ref = next(outs)
            o_ref[...] = _EPILOGUES[ep](y).astype(o_ref.dtype)
        off += wd


def _norm_proj(x2d, g, w, widths, epilogues, dtypes, tm, lbp=None):
    m, d = x2d.shape
    n = w.shape[1]
    tm = min(tm, m)
    assert sum(widths) == n and m % tm == 0
    out_widths = [wd for wd, ep in zip(widths, epilogues) for _ in range(2 if ep == "gates" else 1)]
    assert len(out_widths) == len(dtypes)
    if lbp is None:
        lbp = jnp.zeros((2, LANES), F32)
    return pl.pallas_call(
        functools.partial(_norm_proj_kernel, widths=tuple(widths), epilogues=tuple(epilogues)),
        grid=(m // tm,),
        in_specs=[
            pl.BlockSpec((tm, d), lambda i: (i, 0)),
            pl.BlockSpec((1, d), lambda i: (0, 0)),
            pl.BlockSpec((d, n), lambda i: (0, 0), pipeline_mode=pl.Buffered(1)),
            pl.BlockSpec(lbp.shape, lambda i: (0, 0)),
        ],
        out_specs=[pl.BlockSpec((tm, wd), lambda i: (i, 0)) for wd in out_widths],
        out_shape=[jax.ShapeDtypeStruct((m, wd), dt) for wd, dt in zip(out_widths, dtypes)],
        compiler_params=pltpu.CompilerParams(
            dimension_semantics=("arbitrary",), vmem_limit_bytes=VMEM_LIMIT),
        name="norm_proj",
    )(x2d, g.reshape(1, d).astype(F32), w, lbp)


def _out_proj_kernel(a_ref, m_ref, w_ref, x_ref, g_ref, o_ref, *, final_norm, gated):
    a = a_ref[...]
    if gated:
        a = (a * m_ref[...]).astype(BF16)
    y = x_ref[...] + _dot(a, w_ref[...])
    if final_norm:
        y = _rms(y, g_ref[...])
    o_ref[...] = y


def _out_proj(a2d, w, x2d, g_final, tm, gate=None):
    m, k = a2d.shape
    d = w.shape[1]
    tm = min(tm, m)
    final_norm = g_final is not None
    gated = gate is not None
    if not gated:
        gate = jnp.zeros((8, LANES), F32)
    gate_spec = pl.BlockSpec((tm, k), lambda i: (i, 0)) if gated else pl.BlockSpec((8, LANES), lambda i: (0, 0))
    g = (g_final if final_norm else jnp.ones((d,), F32)).reshape(1, d).astype(F32)
    return pl.pallas_call(
        functools.partial(_out_proj_kernel, final_norm=final_norm, gated=gated),
        grid=(m // tm,),
        in_specs=[
            pl.BlockSpec((tm, k), lambda i: (i, 0)),
            gate_spec,
            pl.BlockSpec((k, d), lambda i: (0, 0), pipeline_mode=pl.Buffered(1)),
            pl.BlockSpec((tm, d), lambda i: (i, 0)),
            pl.BlockSpec((1, d), lambda i: (0, 0)),
        ],
        out_specs=pl.BlockSpec((tm, d), lambda i: (i, 0)),
        out_shape=jax.ShapeDtypeStruct((m, d), F32),
        compiler_params=pltpu.CompilerParams(
            dimension_semantics=("arbitrary",), vmem_limit_bytes=VMEM_LIMIT),
        name="out_proj",
    )(a2d, gate, w, x2d, g)


SUBLANES = 8


def _decay_scores(q, k, f, lvl_ref, chunk):
    nt = chunk // SUBLANES
    tiles = lambda x: [x[SUBLANES * i:SUBLANES * (i + 1)] for i in range(nt)]
    cat = lambda xs: jnp.concatenate(xs, axis=0)
    lvl = lambda i: lvl_ref[SUBLANES * i:SUBLANES * (i + 1), :]
    row = lax.broadcasted_iota(jnp.int32, (SUBLANES, HEAD_A), 0)
    qt, kt = tiles(q), tiles(k)
    et, tt = tiles(f), tiles(f)
    gt = [None] * nt
    k16 = k.astype(BF16)
    prod = _dot_nt(q.astype(BF16), k16)
    att = [jnp.where(lvl(i) == 0, prod[SUBLANES * i:SUBLANES * (i + 1)], 0.0) for i in range(nt)]
    level = 1
    for m in (1, 2, 4):
        a = cat([qt[i] * et[i] for i in range(nt)]).astype(BF16)
        b = k16 if m == 1 else cat([kt[i] * gt[i] for i in range(nt)]).astype(BF16)
        prod = _dot_nt(a, b)
        second = (row & m) != 0
        for i in range(nt):
            att[i] = jnp.where(lvl(i) == level, prod[SUBLANES * i:SUBLANES * (i + 1)], att[i])
            x = jnp.where(second, pltpu.roll(tt[i], m, 0), 1.0)
            y = jnp.where(second, 1.0, pltpu.roll(tt[i], SUBLANES - m, 0))
            et[i] = et[i] * x
            gt[i] = y if gt[i] is None else gt[i] * y
            tt[i] = tt[i] * (x * y)
        level += 1
    mt = 1
    while mt < nt:
        is_second = lambda i: (i // mt) % 2 == 1
        a = cat([qt[i] * et[i] if is_second(i) else qt[i] for i in range(nt)]).astype(BF16)
        b = cat([kt[i] if is_second(i) else kt[i] * gt[i] for i in range(nt)]).astype(BF16)
        prod = _dot_nt(a, b)
        new_t = list(tt)
        for j in range(0, nt, 2 * mt):
            t_first, t_second = tt[j], tt[j + mt]
            both = t_first * t_second
            for i in range(j, j + mt):
                gt[i] = gt[i] * t_second
                new_t[i] = both
            for i in range(j + mt, j + 2 * mt):
                att[i] = jnp.where(lvl(i) == level, prod[SUBLANES * i:SUBLANES * (i + 1)], att[i])
                et[i] = et[i] * t_first
                new_t[i] = both
        tt = new_t
        mt *= 2
        level += 1
    return cat(att), cat(et), cat(gt), tt[0][0:1, :]


def _hgrn_prompt_kernel(q_ref, f_ref, k_ref, v_ref, gs_ref, gon_ref, lvl_ref, *rest,
                        chunk, n_chunks, heads):
    o_ref, s_ref, st_scr, o_scr = rest[-4:]
    tb = pl.program_id(2)

    @pl.when(tb == 0)
    def _():
        st_scr[...] = jnp.zeros_like(st_scr)

    gon = gon_ref[...]
    o_scr[...] = jnp.zeros_like(o_scr)

    def finish(c):
        rows = pl.ds(pl.multiple_of(c * chunk, chunk), chunk)
        for h in range(heads):
            cols = slice(h * HEAD_A, (h + 1) * HEAD_A)
            y = _rms(o_scr[h], gon) * gs_ref[0, rows, cols].astype(F32)
            o_ref[0, rows, cols] = y.astype(o_ref.dtype)

    def body(c, carry):
        finish(jnp.maximum(c - 1, 0))
        rows = pl.ds(pl.multiple_of(c * chunk, chunk), chunk)
        for h in range(heads):
            cols = slice(h * HEAD_A, (h + 1) * HEAD_A)
            q = q_ref[0, rows, cols].astype(F32)
            v = v_ref[0, rows, cols]
            k = k_ref[0, rows, cols]
            att, e, g, tot = _decay_scores(q, k, f_ref[0, rows, cols], lvl_ref, chunk)
            st = st_scr[h]
            o_scr[h] = _dot_nt((q * e).astype(BF16), st.astype(BF16)) + _dot(att.astype(BF16), v)
            st_scr[h] = tot * st + _dot_tn(v, (k * g).astype(BF16))
        return carry

    lax.fori_loop(0, n_chunks, body, 0, unroll=2)
    finish(n_chunks - 1)

    @pl.when(tb == pl.num_programs(2) - 1)
    def _():
        for h in range(heads):
            s_ref[0, 0, h] = st_scr[h].T


def _level_table(chunk):
    t = lax.broadcasted_iota(jnp.int32, (chunk, chunk), 0)
    s = lax.broadcasted_iota(jnp.int32, (chunk, chunk), 1)
    x = t ^ s
    lvl = jnp.zeros((chunk, chunk), jnp.int32)
    m, level = 1, 1
    while m < chunk:
        lvl = jnp.where((x >= m) & (x < 2 * m), level, lvl)
        m *= 2
        level += 1
    return jnp.where(t >= s, lvl, -1)


def _hgrn_prompt(q, f, k, v, gs, gon, states, layer, n_layers, tblk, chunk, heads):
    b, l, di = q.shape
    n_heads = di // HEAD_A
    wd = heads * HEAD_A
    col = lambda i, h, t: (i, t, h)
    in_specs = [
        pl.BlockSpec((1, tblk, wd), col),
        pl.BlockSpec((1, tblk, wd), col),
        pl.BlockSpec((1, tblk, wd), col),
        pl.BlockSpec((1, tblk, wd), col),
        pl.BlockSpec((1, tblk, wd), col),
        pl.BlockSpec((1, HEAD_A), lambda i, h, t: (0, 0)),
        pl.BlockSpec((chunk, chunk), lambda i, h, t: (0, 0)),
    ]
    args = [q, f, k, v, gs, gon.reshape(1, HEAD_A).astype(F32), _level_table(chunk)]
    aliases = {}
    if states is not None:
        in_specs.append(pl.BlockSpec(memory_space=pl.ANY))
        args.append(states)
        aliases = {len(args) - 1: 1}
    return pl.pallas_call(
        functools.partial(_hgrn_prompt_kernel, chunk=chunk, n_chunks=tblk // chunk, heads=heads),
        grid=(b, n_heads // heads, l // tblk),
        in_specs=in_specs,
        out_specs=[
            pl.BlockSpec((1, tblk, wd), col),
            pl.BlockSpec((1, 1, heads, HEAD_A, HEAD_A), lambda i, h, t: (layer, i, h, 0, 0)),
        ],
        out_shape=[
            jax.ShapeDtypeStruct((b, l, di), BF16),
            jax.ShapeDtypeStruct((n_layers, b, n_heads, HEAD_A, HEAD_A), F32),
        ],
        scratch_shapes=[pltpu.VMEM((heads, HEAD_A, HEAD_A), F32), pltpu.VMEM((heads, chunk, HEAD_A), F32)],
        input_output_aliases=aliases,
        compiler_params=pltpu.CompilerParams(
            dimension_semantics=("arbitrary", "arbitrary", "arbitrary"),
            vmem_limit_bytes=VMEM_LIMIT),
        name="hgrn_prompt",
    )(*args)


def _hgrn_sample_kernel(q_ref, f_ref, k_ref, v_ref, gs_ref, s0_ref, gon_ref, *rest, seqs, steps):
    o_ref, s_ref = rest[-2:]
    gon = gon_ref[...]
    row = lax.broadcasted_iota(jnp.int32, (seqs, steps, HEAD_A), 1)
    prow = lax.broadcasted_iota(jnp.int32, (HEAD_A, HEAD_A), 0)
    pad = jnp.zeros((HEAD_A - steps, HEAD_A), F32)
    roll = lambda x, n: pltpu.roll(x, n, 1)

    q = q_ref[...]
    v = v_ref[...]
    f = f_ref[...]
    k = k_ref[...]
    o = jnp.sum(q * k, axis=-1, keepdims=True) * v
    w = f
    for j in range(1, steps):
        p = jnp.where(row >= j, q * roll(k, j) * w, 0.0)
        o = o + jnp.sum(p, axis=-1, keepdims=True) * roll(v, j)
        w = w * roll(f, j)
    e = f
    g = jnp.where(row < steps - 1, roll(f, steps - 1), 1.0)
    sh = 1
    while sh < steps:
        e = e * jnp.where(row >= sh, roll(e, sh), 1.0)
        g = g * jnp.where(row < steps - sh, roll(g, steps - sh), 1.0)
        sh *= 2
    qe = (q * e).astype(BF16)
    kg = k * g
    outs = []
    for i in range(seqs):
        s0 = s0_ref[0, i, 0]
        outs.append(o[i] + _dot(qe[i], s0.astype(BF16)))
        z = jnp.where(prow == steps, e[i, steps - 1:steps, :], jnp.concatenate([kg[i], pad], 0))
        zt = z.T
        vp = jnp.concatenate([v[i], pad], 0)
        s_ref[0, i, 0] = zt[:, steps:steps + 1] * s0 + _dot(zt.astype(BF16), vp.astype(BF16))
    o_ref[...] = (_rms(jnp.stack(outs), gon) * gs_ref[...]).astype(o_ref.dtype)


def _hgrn_sample(q, f, k, v, gs, state_in, gon, states, layer, seqs):
    b, steps, di = q.shape
    heads = di // HEAD_A
    col = lambda i, h: (i, 0, h)
    st = lambda i, h: (layer, i, h, 0, 0)
    in_specs = [
        pl.BlockSpec((seqs, steps, HEAD_A), col),
        pl.BlockSpec((seqs, steps, HEAD_A), col),
        pl.BlockSpec((seqs, steps, HEAD_A), col),
        pl.BlockSpec((seqs, steps, HEAD_A), col),
        pl.BlockSpec((seqs, steps, HEAD_A), col),
        pl.BlockSpec((1, seqs, 1, HEAD_A, HEAD_A), st),
        pl.BlockSpec((1, HEAD_A), lambda i, h: (0, 0)),
    ]
    args = [q, f, k, v, gs, state_in, gon.reshape(1, HEAD_A).astype(F32)]
    aliases = {}
    if states is not None:
        in_specs.append(pl.BlockSpec(memory_space=pl.ANY))
        args.append(states)
        aliases = {len(args) - 1: 1}
    return pl.pallas_call(
        functools.partial(_hgrn_sample_kernel, seqs=seqs, steps=steps),
        grid=(b // seqs, heads),
        in_specs=in_specs,
        out_specs=[
            pl.BlockSpec((seqs, steps, HEAD_A), col),
            pl.BlockSpec((1, seqs, 1, HEAD_A, HEAD_A), st),
        ],
        out_shape=[
            jax.ShapeDtypeStruct((b, steps, di), BF16),
            jax.ShapeDtypeStruct(state_in.shape, F32),
        ],
        input_output_aliases=aliases,
        compiler_params=pltpu.CompilerParams(
            dimension_semantics=("arbitrary", "arbitrary"), vmem_limit_bytes=VMEM_LIMIT),
        name="hgrn_sample",
    )(*args)


def _swa_prompt_kernel(sink_ref, q_ref, kvc_ref, kvp_ref, gs_ref, o_ref, *, kv_heads):
    nb = pl.program_id(1)
    kv2 = jnp.concatenate([kvp_ref[0], kvc_ref[0]], axis=0).astype(BF16)
    kw = kv_heads * HEAD_B
    lane_k = lax.broadcasted_iota(jnp.int32, (2 * WINDOW, LANES), 1)
    lane_q = lax.broadcasted_iota(jnp.int32, (WINDOW, LANES), 1)
    t_i = lax.broadcasted_iota(jnp.int32, (WINDOW, 2 * WINDOW), 0)
    s_i = lax.broadcasted_iota(jnp.int32, (WINDOW, 2 * WINDOW), 1)
    d = t_i + WINDOW - s_i
    valid = (d >= 0) & (d <= WINDOW) & ((s_i >= WINDOW) | (nb > 0))
    zero = jnp.zeros((), BF16)
    pairs_per_group = GROUP_B // 2
    for j in range(kv_heads // 2):
        kk = kv2[:, j * LANES:(j + 1) * LANES]
        vv = kv2[:, kw + j * LANES:kw + (j + 1) * LANES]
        kk_sw = pltpu.roll(kk.astype(F32), HEAD_B, 1).astype(BF16)
        vv_sw = pltpu.roll(vv.astype(F32), HEAD_B, 1).astype(BF16)
        for par in range(2):
            g = 2 * j + par
            if par == 0:
                k2 = jnp.where(lane_k < HEAD_B, kk, kk_sw)
                v_lo = jnp.where(lane_k < HEAD_B, vv, zero)
                v_hi = jnp.where(lane_k >= HEAD_B, vv_sw, zero)
            else:
                k2 = jnp.where(lane_k >= HEAD_B, kk, kk_sw)
                v_lo = jnp.where(lane_k < HEAD_B, vv_sw, zero)
                v_hi = jnp.where(lane_k >= HEAD_B, vv, zero)
            qs, sinks = [], []
            for half in range(2):
                for pr in range(pairs_per_group):
                    pidx = g * pairs_per_group + pr
                    qp = q_ref[0, :, pidx * LANES:(pidx + 1) * LANES]
                    qs.append(jnp.where((lane_q >= HEAD_B) == (half == 1), qp, zero))
                    sinks.append(jnp.full((WINDOW, LANES), sink_ref[2 * pidx + half] * LOG2E, F32))
            sink = jnp.stack(sinks)
            s = _dot_nt(jnp.concatenate(qs, axis=0), k2).reshape(GROUP_B, WINDOW, 2 * WINDOW)
            s = jnp.where(valid[None], s, -jnp.inf)
            mx = jnp.broadcast_to(jnp.max(s, axis=-1, keepdims=True), sink.shape)
            p = jnp.exp2(s - jnp.concatenate([mx, mx], axis=-1))
            den = jnp.sum(p, axis=-1, keepdims=True) + jnp.exp2(sink - mx)
            p = p.astype(BF16).reshape(GROUP_B * WINDOW, 2 * WINDOW)
            rden = (1.0 / den).reshape(2, pairs_per_group * WINDOW, LANES)
            half_rows = pairs_per_group * WINDOW
            o = (_dot(p[:half_rows], v_lo) * rden[0] + _dot(p[half_rows:], v_hi) * rden[1])
            for pr in range(pairs_per_group):
                pidx = g * pairs_per_group + pr
                acc = o[pr * WINDOW:(pr + 1) * WINDOW]
                gs = gs_ref[0, :, pidx * LANES:(pidx + 1) * LANES].astype(F32)
                o_ref[0, :, pidx * LANES:(pidx + 1) * LANES] = (acc * gs).astype(o_ref.dtype)


def _swa_prompt(q, kv, gs, sinks):
    b, l, di = q.shape
    kvw = kv.shape[-1]
    kv_heads = kvw // (2 * HEAD_B)
    return pl.pallas_call(
        functools.partial(_swa_prompt_kernel, kv_heads=kv_heads),
        grid=(b, l // WINDOW),
        in_specs=[
            pl.BlockSpec(memory_space=pltpu.SMEM),
            pl.BlockSpec((1, WINDOW, di), lambda i, n: (i, n, 0)),
            pl.BlockSpec((1, WINDOW, kvw), lambda i, n: (i, n, 0)),
            pl.BlockSpec((1, WINDOW, kvw), lambda i, n: (i, jnp.maximum(n - 1, 0), 0)),
            pl.BlockSpec((1, WINDOW, di), lambda i, n: (i, n, 0)),
        ],
        out_specs=pl.BlockSpec((1, WINDOW, di), lambda i, n: (i, n, 0)),
        out_shape=jax.ShapeDtypeStruct((b, l, di), BF16),
        compiler_params=pltpu.CompilerParams(
            dimension_semantics=("arbitrary", "arbitrary"), vmem_limit_bytes=VMEM_LIMIT),
        name="swa_prompt",
    )(sinks.astype(F32), q, kv, kv, gs)


def _swa_sample_kernel(q_ref, kvn_ref, cache_ref, sk_ref, o_ref, *, seqs, steps, kv_heads):
    kw = kv_heads * HEAD_B
    rows = GROUP_B * steps
    t_i = lax.broadcasted_iota(jnp.int32, (rows, WINDOW), 0) % steps
    c_i = lax.broadcasted_iota(jnp.int32, (rows, WINDOW), 1)
    valid_c = c_i >= t_i
    valid_n = c_i <= t_i
    pad = jnp.zeros((WINDOW - steps, LANES), BF16)
    blocks = [(i, g) for i in range(seqs) for g in range(kv_heads)]
    col = lambda g: slice((g // 2) * LANES, (g // 2 + 1) * LANES)
    vcol = lambda g: slice(kw + (g // 2) * LANES, kw + (g // 2 + 1) * LANES)

    scores = []
    for i, g in blocks:
        q = q_ref[i, g]
        kn = jnp.concatenate([kvn_ref[i, :, col(g)].astype(BF16), pad], 0)
        sc = jnp.where(valid_c, _dot_nt(q, cache_ref[i, :, col(g)].astype(BF16)), -jnp.inf)
        sn = jnp.where(valid_n, _dot_nt(q, kn), -jnp.inf)
        scores.append(jnp.concatenate([sc, sn], axis=1))
    s = jnp.concatenate(scores, axis=0)
    sink = jnp.concatenate([sk_ref[g] for _, g in blocks], axis=0) * LOG2E
    mx = jnp.broadcast_to(jnp.max(s, axis=-1, keepdims=True), sink.shape)
    p = jnp.exp2(s - jnp.concatenate([mx, mx], axis=1))
    den = jnp.sum(p, axis=-1, keepdims=True) + jnp.exp2(sink - mx)
    p = p.astype(BF16)
    for n, (i, g) in enumerate(blocks):
        r = slice(n * rows, (n + 1) * rows)
        vn = jnp.concatenate([kvn_ref[i, :, vcol(g)].astype(BF16), pad], 0)
        o = _dot(p[r, :WINDOW], cache_ref[i, :, vcol(g)].astype(BF16)) + _dot(p[r, WINDOW:], vn)
        o_ref[i, g] = o / den[r]


def _swa_sample(q, kv_new, cache, sinks, seqs):
    b, steps, di = q.shape
    kvw = kv_new.shape[-1]
    kv_heads = kvw // (2 * HEAD_B)
    rows = GROUP_B * steps
    qg = q.reshape(b, steps, kv_heads, GROUP_B, HEAD_B).transpose(0, 2, 3, 1, 4).reshape(b, kv_heads, rows, HEAD_B)
    zeros = jnp.zeros_like(qg)
    odd = (jnp.arange(kv_heads) % 2 == 1)[None, :, None, None]
    qpad = jnp.concatenate([jnp.where(odd, zeros, qg), jnp.where(odd, qg, zeros)], axis=-1).astype(BF16)
    sk = jnp.broadcast_to(sinks.astype(F32).reshape(kv_heads, GROUP_B, 1, 1),
                          (kv_heads, GROUP_B, steps, LANES)).reshape(kv_heads, rows, LANES)
    out = pl.pallas_call(
        functools.partial(_swa_sample_kernel, seqs=seqs, steps=steps, kv_heads=kv_heads),
        grid=(b // seqs,),
        in_specs=[
            pl.BlockSpec((seqs, kv_heads, rows, LANES), lambda i: (i, 0, 0, 0)),
            pl.BlockSpec((seqs, steps, kvw), lambda i: (i, 0, 0)),
            pl.BlockSpec((seqs, WINDOW, kvw), lambda i: (i, 0, 0)),
            pl.BlockSpec((kv_heads, rows, LANES), lambda i: (0, 0, 0)),
        ],
        out_specs=pl.BlockSpec((seqs, kv_heads, rows, LANES), lambda i: (i, 0, 0, 0)),
        out_shape=jax.ShapeDtypeStruct((b, kv_heads, rows, LANES), F32),
        compiler_params=pltpu.CompilerParams(
            dimension_semantics=("arbitrary",), vmem_limit_bytes=VMEM_LIMIT),
        name="swa_sample",
    )(qpad, kv_new, cache, sk)
    og = jnp.where(odd, out[..., HEAD_B:], out[..., :HEAD_B])
    return og.reshape(b, kv_heads, GROUP_B, steps, HEAD_B).transpose(0, 3, 1, 2, 4).reshape(b, steps, di)


def _lower_bound_params(p):
    lb = jnp.cumsum(jax.nn.softmax(p.astype(F32), axis=0), axis=0)
    lb = lb - lb[:1]
    return jnp.stack([lb, 1.0 - lb], axis=1)


def kernel(x_prompt, x_sample, state_hgrn, cache_kv_window, w_in_a, w_out_a, norm_a, onorm_a,
           lower_bounds_a, norm_kv, w_kv, w_in_b, w_out_b, norm_b, sinks_b, norm_f):
    bp, lp, dm = x_prompt.shape
    bs, ls, _ = x_sample.shape
    n_a = w_in_a.shape[0]
    n_b = w_in_b.shape[0]
    di = w_out_a.shape[1]
    kvw = w_kv.shape[1]
    kv_heads = kvw // (2 * HEAD_B)
    wb = cache_kv_window.shape[1]
    assert wb == WINDOW and lp % WINDOW == 0 and ls <= WINDOW

    lbp = _lower_bound_params(lower_bounds_a)
    hp = x_prompt.reshape(bp * lp, dm)
    hs = x_sample.reshape(bs * ls, dm)
    cache = cache_kv_window.reshape(bs, wb, kvw)

    state_prompt = state_sample = None
    for layer in range(n_a):
        w_in = w_in_a[layer].astype(BF16)
        w_out = w_out_a[layer].astype(BF16)
        splits = dict(widths=(di,) * 4, epilogues=("silu_a", "gates", "none", "silu"), lbp=lbp[layer])
        parts = _norm_proj(hp, norm_a[layer], w_in, dtypes=(BF16, F32, F32, BF16, BF16), tm=256, **splits)
        og, state_prompt = _hgrn_prompt(
            *(a.reshape(bp, lp, di) for a in parts), onorm_a[layer],
            state_prompt, layer, n_a, tblk=min(512, lp), chunk=128, heads=4)
        hp = _out_proj(og.reshape(bp * lp, di), w_out, hp, None, tm=512)

        parts = _norm_proj(hs, norm_a[layer], w_in, dtypes=(F32,) * 5, tm=256, **splits)
        og, state_sample = _hgrn_sample(
            *(a.reshape(bs, ls, di) for a in parts), state_hgrn, onorm_a[layer],
            state_sample, layer, seqs=min(16, bs))
        hs = _out_proj(og.reshape(bs * ls, di), w_out, hs, None, tm=512)

    w_kv_b = w_kv.astype(BF16)
    (kv_p,) = _norm_proj(hp, norm_kv, w_kv_b, (kvw,), ("none",), (F32,), tm=512)
    (kv_s,) = _norm_proj(hs, norm_kv, w_kv_b, (kvw,), ("none",), (F32,), tm=512)
    kv_p = kv_p.reshape(bp, lp, kvw)
    kv_s = kv_s.reshape(bs, ls, kvw)

    for j in range(n_b):
        w_in = w_in_b[j].astype(BF16)
        w_out = w_out_b[j].astype(BF16)
        g_final = norm_f if j == n_b - 1 else None
        splits = dict(widths=(di, di), epilogues=("scale_b", "silu"))
        q, gs = _norm_proj(hp, norm_b[j], w_in, dtypes=(BF16, BF16), tm=512, **splits)
        o = _swa_prompt(q.reshape(bp, lp, di), kv_p, gs.reshape(bp, lp, di), sinks_b[j])
        hp = _out_proj(o.reshape(bp * lp, di), w_out, hp, g_final, tm=512)

        q, gs = _norm_proj(hs, norm_b[j], w_in, dtypes=(F32, F32), tm=512, **splits)
        o = _swa_sample(q.reshape(bs, ls, di), kv_s, cache, sinks_b[j], seqs=8)
        hs = _out_proj(o.reshape(bs * ls, di), w_out, hs, g_final, tm=512, gate=gs)

    kv_shape = (2, kv_heads, HEAD_B)
    y_prompt = hp.reshape(bp, lp, dm)
    y_sample = hs.reshape(bs, ls, dm)
    kv_window_prompt = kv_p[:, lp - min(WINDOW, lp):].reshape(bp, min(WINDOW, lp), *kv_shape)
    kv_window_sample = jnp.concatenate([cache, kv_s], axis=1)[:, ls:].reshape(bs, wb, *kv_shape)
    return (y_prompt, y_sample, state_prompt, state_sample, kv_window_prompt, kv_window_sample)
```

```python
import functools

import jax
import jax.numpy as jnp
from jax import lax
from jax.experimental import pallas as pl
from jax.experimental.pallas import tpu as pltpu

F32 = jnp.float32
BF16 = jnp.bfloat16

EPS = 1e-6
HEAD_A = 128
HEAD_B = 64
GROUP_B = 8
WINDOW = 128
LANES = 128
VMEM_LIMIT = 56 * 1024 * 1024
LOG2E = 1.4426950408889634


def _dot(a, b):
    return jnp.dot(a, b, preferred_element_type=F32)


def _dot_nt(a, b):
    return lax.dot_general(a, b, (((1,), (1,)), ((), ())), preferred_element_type=F32)


def _dot_tn(a, b):
    return lax.dot_general(a, b, (((0,), (0,)), ((), ())), preferred_element_type=F32)


def _silu(x):
    return x * (0.5 * jnp.tanh(0.5 * x) + 0.5)


def _rms(x, g):
    return x * lax.rsqrt(jnp.mean(x * x, axis=-1, keepdims=True) + EPS) * g


def _gates(fz, lb, one_m_lb):
    e = jnp.exp(-jnp.abs(fz))
    r = 1.0 / (1.0 + e)
    er = e * r
    pos = fz >= 0.0
    f = lb + one_m_lb * jnp.where(pos, r, er)
    key = one_m_lb * jnp.where(pos, er, r)
    return f, key


_EPILOGUES = {
    "none": lambda y: y,
    "silu": _silu,
    "silu_a": lambda y: _silu(y) * HEAD_A ** -0.5,
    "scale_b": lambda y: y * (HEAD_B ** -0.5 * LOG2E),
}


def _bf16_weight(w_ref, scr):
    if not scr:
        return w_ref

    @pl.when(pl.program_id(0) == 0)
    def _():
        scr[0][...] = w_ref[...].astype(BF16)

    return scr[0]


def _weight_scratch(w):
    return [] if w.dtype == BF16 else [pltpu.VMEM(w.shape, BF16)]


def _norm_proj_kernel(x_ref, g_ref, w_ref, *rest, widths, epilogues):
    out_refs, scr = rest[:len(widths)], rest[len(widths):]
    w = _bf16_weight(w_ref, scr)
    h = _rms(x_ref[...], g_ref[...]).astype(BF16)
    off = 0
    for o_ref, wd, ep in zip(out_refs, widths, epilogues):
        o_ref[...] = _EPILOGUES[ep](_dot(h, w[:, off:off + wd])).astype(o_ref.dtype)
        off += wd


def _norm_proj(x2d, g, w, widths, epilogues, dtypes, tm):
    m, d = x2d.shape
    n = w.shape[1]
    tm = min(tm, m)
    assert sum(widths) == n and m % tm == 0
    return pl.pallas_call(
        functools.partial(_norm_proj_kernel, widths=tuple(widths), epilogues=tuple(epilogues)),
        grid=(m // tm,),
        in_specs=[
            pl.BlockSpec((tm, d), lambda i: (i, 0)),
            pl.BlockSpec((1, d), lambda i: (0, 0)),
            pl.BlockSpec((d, n), lambda i: (0, 0), pipeline_mode=pl.Buffered(1)),
        ],
        out_specs=[pl.BlockSpec((tm, wd), lambda i: (i, 0)) for wd in widths],
        out_shape=[jax.ShapeDtypeStruct((m, wd), dt) for wd, dt in zip(widths, dtypes)],
        scratch_shapes=_weight_scratch(w),
        compiler_params=pltpu.CompilerParams(
            dimension_semantics=("arbitrary",), vmem_limit_bytes=VMEM_LIMIT),
        name="norm_proj",
    )(x2d, g.reshape(1, d).astype(F32), w)


def _out_proj_kernel(a_ref, m_ref, w_ref, x_ref, g_ref, w2_ref, *rest, mode, gated):
    n_out = 2 if mode == "project" else 1
    out_refs, scr = rest[:n_out], rest[n_out:]
    a = a_ref[...]
    if gated:
        a = (a * m_ref[...]).astype(BF16)
    y = x_ref[...] + _dot(a, _bf16_weight(w_ref, scr)[...])
    if mode == "norm":
        y = _rms(y, g_ref[...])
    out_refs[0][...] = y
    if mode == "project":
        out_refs[1][...] = _dot(_rms(y, g_ref[...]).astype(BF16), w2_ref[...].astype(BF16))


def _out_proj(a2d, w, x2d, tm, gate=None, norm=None, project=None):
    m, k = a2d.shape
    d = w.shape[1]
    tm = min(tm, m)
    gated = gate is not None
    if not gated:
        gate = jnp.zeros((8, LANES), F32)
    gate_spec = pl.BlockSpec((tm, k), lambda i: (i, 0)) if gated else pl.BlockSpec((8, LANES), lambda i: (0, 0))
    mode = "norm" if norm is not None else "project" if project is not None else "plain"
    g, w2 = (norm, None) if mode == "norm" else project if mode == "project" else (jnp.ones((d,), F32), None)
    if w2 is None:
        w2 = jnp.zeros((8, LANES), F32)
    out_specs = [pl.BlockSpec((tm, d), lambda i: (i, 0))]
    out_shape = [jax.ShapeDtypeStruct((m, d), F32)]
    if mode == "project":
        out_specs.append(pl.BlockSpec((tm, w2.shape[1]), lambda i: (i, 0)))
        out_shape.append(jax.ShapeDtypeStruct((m, w2.shape[1]), F32))
    outs = pl.pallas_call(
        functools.partial(_out_proj_kernel, mode=mode, gated=gated),
        grid=(m // tm,),
        in_specs=[
            pl.BlockSpec((tm, k), lambda i: (i, 0)),
            gate_spec,
            pl.BlockSpec((k, d), lambda i: (0, 0), pipeline_mode=pl.Buffered(1)),
            pl.BlockSpec((tm, d), lambda i: (i, 0)),
            pl.BlockSpec((1, d), lambda i: (0, 0)),
            pl.BlockSpec(w2.shape, lambda i: (0, 0), pipeline_mode=pl.Buffered(1)),
        ],
        out_specs=out_specs,
        out_shape=out_shape,
        scratch_shapes=_weight_scratch(w),
        compiler_params=pltpu.CompilerParams(
            dimension_semantics=("arbitrary",), vmem_limit_bytes=VMEM_LIMIT),
        name="out_proj",
    )(a2d, gate, w, x2d, g.reshape(1, d).astype(F32), w2)
    return outs if mode == "project" else outs[0]


SUBLANES = 8


def _decay_scores(q, k, f, lvl_ref, chunk):
    nt = chunk // SUBLANES
    tiles = lambda x: [x[SUBLANES * i:SUBLANES * (i + 1)] for i in range(nt)]
    cat = lambda xs: jnp.concatenate(xs, axis=0)
    lvl = lambda i: lvl_ref[SUBLANES * i:SUBLANES * (i + 1), :]
    row = lax.broadcasted_iota(jnp.int32, (SUBLANES, HEAD_A), 0)
    qt, kt = tiles(q), tiles(k)
    et, tt = tiles(f), tiles(f)
    gt = [None] * nt
    k16 = k.astype(BF16)
    prod = _dot_nt(q.astype(BF16), k16)
    att = [jnp.where(lvl(i) == 0, prod[SUBLANES * i:SUBLANES * (i + 1)], 0.0) for i in range(nt)]
    level = 1
    for m in (1, 2, 4):
        a = cat([qt[i] * et[i] for i in range(nt)]).astype(BF16)
        b = k16 if m == 1 else cat([kt[i] * gt[i] for i in range(nt)]).astype(BF16)
        prod = _dot_nt(a, b)
        second = (row & m) != 0
        for i in range(nt):
            att[i] = jnp.where(lvl(i) == level, prod[SUBLANES * i:SUBLANES * (i + 1)], att[i])
            x = jnp.where(second, pltpu.roll(tt[i], m, 0), 1.0)
            y = jnp.where(second, 1.0, pltpu.roll(tt[i], SUBLANES - m, 0))
            et[i] = et[i] * x
            gt[i] = y if gt[i] is None else gt[i] * y
            tt[i] = tt[i] * (x * y)
        level += 1
    mt = 1
    while mt < nt:
        is_second = lambda i: (i // mt) % 2 == 1
        a = cat([qt[i] * et[i] if is_second(i) else qt[i] for i in range(nt)]).astype(BF16)
        b = cat([kt[i] if is_second(i) else kt[i] * gt[i] for i in range(nt)]).astype(BF16)
        prod = _dot_nt(a, b)
        new_t = list(tt)
        for j in range(0, nt, 2 * mt):
            t_first, t_second = tt[j], tt[j + mt]
            both = t_first * t_second
            for i in range(j, j + mt):
                gt[i] = gt[i] * t_second
                new_t[i] = both
            for i in range(j + mt, j + 2 * mt):
                att[i] = jnp.where(lvl(i) == level, prod[SUBLANES * i:SUBLANES * (i + 1)], att[i])
                et[i] = et[i] * t_first
                new_t[i] = both
        tt = new_t
        mt *= 2
        level += 1
    return cat(att), cat(et), cat(gt), tt[0][0:1, :]


def _hgrn_prompt_kernel(q_ref, fz_ref, v_ref, gs_ref, lbp_ref, gon_ref, lvl_ref, *rest,
                        chunk, n_chunks, heads):
    o_ref, s_ref, st_scr, o_scr = rest[-4:]
    tb = pl.program_id(2)

    @pl.when(tb == 0)
    def _():
        st_scr[...] = jnp.zeros_like(st_scr)

    gon = gon_ref[...]
    o_scr[...] = jnp.zeros_like(o_scr)

    def finish(c):
        rows = pl.ds(pl.multiple_of(c * chunk, chunk), chunk)
        for h in range(heads):
            cols = slice(h * HEAD_A, (h + 1) * HEAD_A)
            y = _rms(o_scr[h], gon) * gs_ref[0, rows, cols].astype(F32)
            o_ref[0, rows, cols] = y.astype(o_ref.dtype)

    def body(c, carry):
        finish(jnp.maximum(c - 1, 0))
        rows = pl.ds(pl.multiple_of(c * chunk, chunk), chunk)
        for h in range(heads):
            cols = slice(h * HEAD_A, (h + 1) * HEAD_A)
            q = q_ref[0, rows, cols].astype(F32)
            v = v_ref[0, rows, cols]
            f, k = _gates(fz_ref[0, rows, cols], lbp_ref[0:1, cols], lbp_ref[1:2, cols])
            att, e, g, tot = _decay_scores(q, k, f, lvl_ref, chunk)
            st = st_scr[h]
            o_scr[h] = _dot_nt((q * e).astype(BF16), st.astype(BF16)) + _dot(att.astype(BF16), v)
            st_scr[h] = tot * st + _dot_tn(v, (k * g).astype(BF16))
        return carry

    lax.fori_loop(0, n_chunks, body, 0)
    finish(n_chunks - 1)

    @pl.when(tb == pl.num_programs(2) - 1)
    def _():
        for h in range(heads):
            s_ref[0, 0, h] = st_scr[h].T


def _level_table(chunk):
    t = lax.broadcasted_iota(jnp.int32, (chunk, chunk), 0)
    s = lax.broadcasted_iota(jnp.int32, (chunk, chunk), 1)
    x = t ^ s
    lvl = jnp.zeros((chunk, chunk), jnp.int32)
    m, level = 1, 1
    while m < chunk:
        lvl = jnp.where((x >= m) & (x < 2 * m), level, lvl)
        m *= 2
        level += 1
    return jnp.where(t >= s, lvl, -1)


def _hgrn_prompt(q, fz, v, gs, lbp, gon, states, layer, n_layers, tblk, chunk, heads):
    b, l, di = q.shape
    n_heads = di // HEAD_A
    wd = heads * HEAD_A
    col = lambda i, h, t: (i, t, h)
    in_specs = [
        pl.BlockSpec((1, tblk, wd), col),
        pl.BlockSpec((1, tblk, wd), col),
        pl.BlockSpec((1, tblk, wd), col),
        pl.BlockSpec((1, tblk, wd), col),
        pl.BlockSpec((2, wd), lambda i, h, t: (0, h)),
        pl.BlockSpec((1, HEAD_A), lambda i, h, t: (0, 0)),
        pl.BlockSpec((chunk, chunk), lambda i, h, t: (0, 0)),
    ]
    args = [q, fz, v, gs, lbp, gon.reshape(1, HEAD_A).astype(F32), _level_table(chunk)]
    aliases = {}
    if states is not None:
        in_specs.append(pl.BlockSpec(memory_space=pl.ANY))
        args.append(states)
        aliases = {len(args) - 1: 1}
    return pl.pallas_call(
        functools.partial(_hgrn_prompt_kernel, chunk=chunk, n_chunks=tblk // chunk, heads=heads),
        grid=(b, n_heads // heads, l // tblk),
        in_specs=in_specs,
        out_specs=[
            pl.BlockSpec((1, tblk, wd), col),
            pl.BlockSpec((1, 1, heads, HEAD_A, HEAD_A), lambda i, h, t: (layer, i, h, 0, 0)),
        ],
        out_shape=[
            jax.ShapeDtypeStruct((b, l, di), BF16),
            jax.ShapeDtypeStruct((n_layers, b, n_heads, HEAD_A, HEAD_A), F32),
        ],
        scratch_shapes=[pltpu.VMEM((heads, HEAD_A, HEAD_A), F32), pltpu.VMEM((heads, chunk, HEAD_A), F32)],
        input_output_aliases=aliases,
        compiler_params=pltpu.CompilerParams(
            dimension_semantics=("arbitrary", "arbitrary", "arbitrary"),
            vmem_limit_bytes=VMEM_LIMIT),
        name="hgrn_prompt",
    )(*args)


def _hgrn_sample_kernel(q_ref, fz_ref, v_ref, gs_ref, s0_ref, lbp_ref, gon_ref, *rest, seqs, steps):
    o_ref, s_ref = rest[-2:]
    gon = gon_ref[...]
    row = lax.broadcasted_iota(jnp.int32, (seqs, steps, HEAD_A), 1)
    prow = lax.broadcasted_iota(jnp.int32, (HEAD_A, HEAD_A), 0)
    pad = jnp.zeros((HEAD_A - steps, HEAD_A), F32)
    roll = lambda x, n: pltpu.roll(x, n, 1)

    q = q_ref[...]
    v = v_ref[...]
    f, k = _gates(fz_ref[...], lbp_ref[0:1, :], lbp_ref[1:2, :])
    o = jnp.sum(q * k, axis=-1, keepdims=True) * v
    w = f
    for j in range(1, steps):
        p = jnp.where(row >= j, q * roll(k, j) * w, 0.0)
        o = o + jnp.sum(p, axis=-1, keepdims=True) * roll(v, j)
        w = w * roll(f, j)
    e = f
    g = jnp.where(row < steps - 1, roll(f, steps - 1), 1.0)
    sh = 1
    while sh < steps:
        e = e * jnp.where(row >= sh, roll(e, sh), 1.0)
        g = g * jnp.where(row < steps - sh, roll(g, steps - sh), 1.0)
        sh *= 2
    qe = (q * e).astype(BF16)
    kg = k * g
    outs = []
    for i in range(seqs):
        s0 = s0_ref[0, i, 0]
        outs.append(o[i] + _dot(qe[i], s0.astype(BF16)))
        z = jnp.where(prow == steps, e[i, steps - 1:steps, :], jnp.concatenate([kg[i], pad], 0))
        zt = z.T
        vp = jnp.concatenate([v[i], pad], 0)
        s_ref[0, i, 0] = zt[:, steps:steps + 1] * s0 + _dot(zt.astype(BF16), vp.astype(BF16))
    o_ref[...] = (_rms(jnp.stack(outs), gon) * gs_ref[...]).astype(o_ref.dtype)


def _hgrn_sample(q, fz, v, gs, state_in, lbp, gon, states, layer, seqs):
    b, steps, di = q.shape
    heads = di // HEAD_A
    col = lambda i, h: (i, 0, h)
    st = lambda i, h: (layer, i, h, 0, 0)
    in_specs = [
        pl.BlockSpec((seqs, steps, HEAD_A), col),
        pl.BlockSpec((seqs, steps, HEAD_A), col),
        pl.BlockSpec((seqs, steps, HEAD_A), col),
        pl.BlockSpec((seqs, steps, HEAD_A), col),
        pl.BlockSpec((1, seqs, 1, HEAD_A, HEAD_A), st),
        pl.BlockSpec((2, HEAD_A), lambda i, h: (0, h)),
        pl.BlockSpec((1, HEAD_A), lambda i, h: (0, 0)),
    ]
    args = [q, fz, v, gs, state_in, lbp, gon.reshape(1, HEAD_A).astype(F32)]
    aliases = {}
    if states is not None:
        in_specs.append(pl.BlockSpec(memory_space=pl.ANY))
        args.append(states)
        aliases = {len(args) - 1: 1}
    return pl.pallas_call(
        functools.partial(_hgrn_sample_kernel, seqs=seqs, steps=steps),
        grid=(b // seqs, heads),
        in_specs=in_specs,
        out_specs=[
            pl.BlockSpec((seqs, steps, HEAD_A), col),
            pl.BlockSpec((1, seqs, 1, HEAD_A, HEAD_A), st),
        ],
        out_shape=[
            jax.ShapeDtypeStruct((b, steps, di), BF16),
            jax.ShapeDtypeStruct(state_in.shape, F32),
        ],
        input_output_aliases=aliases,
        compiler_params=pltpu.CompilerParams(
            dimension_semantics=("arbitrary", "arbitrary"), vmem_limit_bytes=VMEM_LIMIT),
        name="hgrn_sample",
    )(*args)


def _swa_prompt_kernel(sink_ref, q_ref, kvc_ref, kvp_ref, gs_ref, o_ref, *, kv_heads):
    nb = pl.program_id(1)
    kv2 = jnp.concatenate([kvp_ref[0], kvc_ref[0]], axis=0).astype(BF16)
    kw = kv_heads * HEAD_B
    lane_k = lax.broadcasted_iota(jnp.int32, (2 * WINDOW, LANES), 1)
    lane_q = lax.broadcasted_iota(jnp.int32, (WINDOW, LANES), 1)
    t_i = lax.broadcasted_iota(jnp.int32, (WINDOW, 2 * WINDOW), 0)
    s_i = lax.broadcasted_iota(jnp.int32, (WINDOW, 2 * WINDOW), 1)
    d = t_i + WINDOW - s_i
    valid = (d >= 0) & (d <= WINDOW) & ((s_i >= WINDOW) | (nb > 0))
    zero = jnp.zeros((), BF16)
    pairs_per_group = GROUP_B // 2
    for j in range(kv_heads // 2):
        kk = kv2[:, j * LANES:(j + 1) * LANES]
        vv = kv2[:, kw + j * LANES:kw + (j + 1) * LANES]
        kk_sw = pltpu.roll(kk.astype(F32), HEAD_B, 1).astype(BF16)
        vv_sw = pltpu.roll(vv.astype(F32), HEAD_B, 1).astype(BF16)
        for par in range(2):
            g = 2 * j + par
            if par == 0:
                k2 = jnp.where(lane_k < HEAD_B, kk, kk_sw)
                v_lo = jnp.where(lane_k < HEAD_B, vv, zero)
                v_hi = jnp.where(lane_k >= HEAD_B, vv_sw, zero)
            else:
                k2 = jnp.where(lane_k >= HEAD_B, kk, kk_sw)
                v_lo = jnp.where(lane_k < HEAD_B, vv_sw, zero)
                v_hi = jnp.where(lane_k >= HEAD_B, vv, zero)
            qs, sinks = [], []
            for half in range(2):
                for pr in range(pairs_per_group):
                    pidx = g * pairs_per_group + pr
                    qp = q_ref[0, :, pidx * LANES:(pidx + 1) * LANES]
                    qs.append(jnp.where((lane_q >= HEAD_B) == (half == 1), qp, zero))
                    sinks.append(jnp.full((WINDOW, LANES), sink_ref[2 * pidx + half] * LOG2E, F32))
            sink = jnp.stack(sinks)
            s = _dot_nt(jnp.concatenate(qs, axis=0), k2).reshape(GROUP_B, WINDOW, 2 * WINDOW)
            s = jnp.where(valid[None], s, -jnp.inf)
            mx = jnp.broadcast_to(jnp.max(s, axis=-1, keepdims=True), sink.shape)
            p = jnp.exp2(s - jnp.concatenate([mx, mx], axis=-1))
            den = jnp.sum(p, axis=-1, keepdims=True) + jnp.exp2(sink - mx)
            p = p.astype(BF16).reshape(GROUP_B * WINDOW, 2 * WINDOW)
            rden = (1.0 / den).reshape(2, pairs_per_group * WINDOW, LANES)
            half_rows = pairs_per_group * WINDOW
            o = (_dot(p[:half_rows], v_lo) * rden[0] + _dot(p[half_rows:], v_hi) * rden[1])
            for pr in range(pairs_per_group):
                pidx = g * pairs_per_group + pr
                acc = o[pr * WINDOW:(pr + 1) * WINDOW]
                gs = gs_ref[0, :, pidx * LANES:(pidx + 1) * LANES].astype(F32)
                o_ref[0, :, pidx * LANES:(pidx + 1) * LANES] = (acc * gs).astype(o_ref.dtype)


def _swa_prompt(q, kv, gs, sinks):
    b, l, di = q.shape
    kvw = kv.shape[-1]
    kv_heads = kvw // (2 * HEAD_B)
    return pl.pallas_call(
        functools.partial(_swa_prompt_kernel, kv_heads=kv_heads),
        grid=(b, l // WINDOW),
        in_specs=[
            pl.BlockSpec(memory_space=pltpu.SMEM),
            pl.BlockSpec((1, WINDOW, di), lambda i, n: (i, n, 0)),
            pl.BlockSpec((1, WINDOW, kvw), lambda i, n: (i, n, 0)),
            pl.BlockSpec((1, WINDOW, kvw), lambda i, n: (i, jnp.maximum(n - 1, 0), 0)),
            pl.BlockSpec((1, WINDOW, di), lambda i, n: (i, n, 0)),
        ],
        out_specs=pl.BlockSpec((1, WINDOW, di), lambda i, n: (i, n, 0)),
        out_shape=jax.ShapeDtypeStruct((b, l, di), BF16),
        compiler_params=pltpu.CompilerParams(
            dimension_semantics=("arbitrary", "arbitrary"), vmem_limit_bytes=VMEM_LIMIT),
        name="swa_prompt",
    )(sinks.astype(F32), q, kv, kv, gs)


def _swa_sample_kernel(q_ref, kvn_ref, cache_ref, sk_ref, o_ref, *, seqs, steps, kv_heads):
    kw = kv_heads * HEAD_B
    rows = GROUP_B * steps
    t_i = lax.broadcasted_iota(jnp.int32, (rows, WINDOW), 0) % steps
    c_i = lax.broadcasted_iota(jnp.int32, (rows, WINDOW), 1)
    valid_c = c_i >= t_i
    valid_n = c_i <= t_i
    pad = jnp.zeros((WINDOW - steps, LANES), BF16)
    blocks = [(i, g) for i in range(seqs) for g in range(kv_heads)]
    col = lambda g: slice((g // 2) * LANES, (g // 2 + 1) * LANES)
    vcol = lambda g: slice(kw + (g // 2) * LANES, kw + (g // 2 + 1) * LANES)

    scores = []
    for i, g in blocks:
        q = q_ref[i, g]
        kn = jnp.concatenate([kvn_ref[i, :, col(g)].astype(BF16), pad], 0)
        sc = jnp.where(valid_c, _dot_nt(q, cache_ref[i, :, col(g)].astype(BF16)), -jnp.inf)
        sn = jnp.where(valid_n, _dot_nt(q, kn), -jnp.inf)
        scores.append(jnp.concatenate([sc, sn], axis=1))
    s = jnp.concatenate(scores, axis=0)
    sink = jnp.concatenate([sk_ref[g] for _, g in blocks], axis=0) * LOG2E
    mx = jnp.broadcast_to(jnp.max(s, axis=-1, keepdims=True), sink.shape)
    p = jnp.exp2(s - jnp.concatenate([mx, mx], axis=1))
    den = jnp.sum(p, axis=-1, keepdims=True) + jnp.exp2(sink - mx)
    p = p.astype(BF16)
    for n, (i, g) in enumerate(blocks):
        r = slice(n * rows, (n + 1) * rows)
        vn = jnp.concatenate([kvn_ref[i, :, vcol(g)].astype(BF16), pad], 0)
        o = _dot(p[r, :WINDOW], cache_ref[i, :, vcol(g)].astype(BF16)) + _dot(p[r, WINDOW:], vn)
        o_ref[i, g] = o / den[r]


def _swa_sample(q, kv_new, cache, sinks, seqs):
    b, steps, di = q.shape
    kvw = kv_new.shape[-1]
    kv_heads = kvw // (2 * HEAD_B)
    rows = GROUP_B * steps
    qg = q.reshape(b, steps, kv_heads, GROUP_B, HEAD_B).transpose(0, 2, 3, 1, 4).reshape(b, kv_heads, rows, HEAD_B)
    zeros = jnp.zeros_like(qg)
    odd = (jnp.arange(kv_heads) % 2 == 1)[None, :, None, None]
    qpad = jnp.concatenate([jnp.where(odd, zeros, qg), jnp.where(odd, qg, zeros)], axis=-1).astype(BF16)
    sk = jnp.broadcast_to(sinks.astype(F32).reshape(kv_heads, GROUP_B, 1, 1),
                          (kv_heads, GROUP_B, steps, LANES)).reshape(kv_heads, rows, LANES)
    out = pl.pallas_call(
        functools.partial(_swa_sample_kernel, seqs=seqs, steps=steps, kv_heads=kv_heads),
        grid=(b // seqs,),
        in_specs=[
            pl.BlockSpec((seqs, kv_heads, rows, LANES), lambda i: (i, 0, 0, 0)),
            pl.BlockSpec((seqs, steps, kvw), lambda i: (i, 0, 0)),
            pl.BlockSpec((seqs, WINDOW, kvw), lambda i: (i, 0, 0)),
            pl.BlockSpec((kv_heads, rows, LANES), lambda i: (0, 0, 0)),
        ],
        out_specs=pl.BlockSpec((seqs, kv_heads, rows, LANES), lambda i: (i, 0, 0, 0)),
        out_shape=jax.ShapeDtypeStruct((b, kv_heads, rows, LANES), F32),
        compiler_params=pltpu.CompilerParams(
            dimension_semantics=("arbitrary",), vmem_limit_bytes=VMEM_LIMIT),
        name="swa_sample",
    )(qpad, kv_new, cache, sk)
    og = jnp.where(odd, out[..., HEAD_B:], out[..., :HEAD_B])
    return og.reshape(b, kv_heads, GROUP_B, steps, HEAD_B).transpose(0, 3, 1, 2, 4).reshape(b, steps, di)


def _lower_bound_params(p):
    lb = jnp.cumsum(jax.nn.softmax(p.astype(F32), axis=0), axis=0)
    lb = lb - lb[:1]
    return jnp.stack([lb, 1.0 - lb], axis=1)


def kernel(x_prompt, x_sample, state_hgrn, cache_kv_window, w_in_a, w_out_a, norm_a, onorm_a,
           lower_bounds_a, norm_kv, w_kv, w_in_b, w_out_b, norm_b, sinks_b, norm_f):
    bp, lp, dm = x_prompt.shape
    bs, ls, _ = x_sample.shape
    n_a = w_in_a.shape[0]
    n_b = w_in_b.shape[0]
    di = w_out_a.shape[1]
    kvw = w_kv.shape[1]
    kv_heads = kvw // (2 * HEAD_B)
    wb = cache_kv_window.shape[1]
    assert wb == WINDOW and lp % WINDOW == 0 and ls <= WINDOW

    lbp = _lower_bound_params(lower_bounds_a)
    hp = x_prompt.reshape(bp * lp, dm)
    hs = x_sample.reshape(bs * ls, dm)
    cache = cache_kv_window.reshape(bs, wb, kvw)

    state_prompt = state_sample = None
    kv_p = kv_s = None
    for layer in range(n_a):
        w_in = w_in_a[layer].astype(BF16)
        w_out = w_out_a[layer]
        project = (norm_kv, w_kv) if layer == n_a - 1 else None
        splits = dict(widths=(di,) * 4, epilogues=("silu_a", "none", "none", "silu"))
        parts = _norm_proj(hp, norm_a[layer], w_in, dtypes=(BF16, F32, BF16, BF16), tm=256, **splits)
        og, state_prompt = _hgrn_prompt(
            *(a.reshape(bp, lp, di) for a in parts), lbp[layer], onorm_a[layer],
            state_prompt, layer, n_a, tblk=min(1024, lp), chunk=min(256, lp), heads=4)
        hp = _out_proj(og.reshape(bp * lp, di), w_out, hp, tm=512, project=project)

        parts = _norm_proj(hs, norm_a[layer], w_in, dtypes=(F32,) * 4, tm=256, **splits)
        og, state_sample = _hgrn_sample(
            *(a.reshape(bs, ls, di) for a in parts), state_hgrn, lbp[layer], onorm_a[layer],
            state_sample, layer, seqs=min(16, bs))
        hs = _out_proj(og.reshape(bs * ls, di), w_out, hs, tm=512, project=project)
    hp, kv_p = hp
    hs, kv_s = hs
    kv_p = kv_p.reshape(bp, lp, kvw)
    kv_s = kv_s.reshape(bs, ls, kvw)

    for j in range(n_b):
        g_final = norm_f if j == n_b - 1 else None
        splits = dict(widths=(di, di), epilogues=("scale_b", "silu"))
        q, gs = _norm_proj(hp, norm_b[j], w_in_b[j], dtypes=(BF16, BF16), tm=512, **splits)
        o = _swa_prompt(q.reshape(bp, lp, di), kv_p, gs.reshape(bp, lp, di), sinks_b[j])
        hp = _out_proj(o.reshape(bp * lp, di), w_out_b[j], hp, tm=512, norm=g_final)

        q, gs = _norm_proj(hs, norm_b[j], w_in_b[j], dtypes=(F32, F32), tm=512, **splits)
        o = _swa_sample(q.reshape(bs, ls, di), kv_s, cache, sinks_b[j], seqs=8)
        hs = _out_proj(o.reshape(bs * ls, di), w_out_b[j], hs, tm=512, gate=gs, norm=g_final)

    kv_shape = (2, kv_heads, HEAD_B)
    y_prompt = hp.reshape(bp, lp, dm)
    y_sample = hs.reshape(bs, ls, dm)
    kv_window_prompt = kv_p[:, lp - min(WINDOW, lp):].reshape(bp, min(WINDOW, lp), *kv_shape)
    kv_window_sample = jnp.concatenate([cache, kv_s], axis=1)[:, ls:].reshape(bs, wb, *kv_shape)
    return (y_prompt, y_sample, state_prompt, state_sample, kv_window_prompt, kv_window_sample)
```

```python
import functools

import jax
import jax.numpy as jnp
from jax import lax
from jax.experimental import pallas as pl
from jax.experimental.pallas import tpu as pltpu

F32 = jnp.float32
BF16 = jnp.bfloat16

EPS = 1e-6
HEAD_A = 128
HEAD_B = 64
GROUP_B = 8
WINDOW = 128
LANES = 128
VMEM_LIMIT = 56 * 1024 * 1024
LOG2E = 1.4426950408889634


def _dot(a, b):
    return jnp.dot(a, b, preferred_element_type=F32)


def _dot_nt(a, b):
    return lax.dot_general(a, b, (((1,), (1,)), ((), ())), preferred_element_type=F32)


def _dot_tn(a, b):
    return lax.dot_general(a, b, (((0,), (0,)), ((), ())), preferred_element_type=F32)


def _silu(x):
    return x * (0.5 * jnp.tanh(0.5 * x) + 0.5)


def _rms(x, g):
    return x * lax.rsqrt(jnp.mean(x * x, axis=-1, keepdims=True) + EPS) * g


def _gates(fz, lb, one_m_lb):
    e = jnp.exp(-jnp.abs(fz))
    r = 1.0 / (1.0 + e)
    er = e * r
    pos = fz >= 0.0
    f = lb + one_m_lb * jnp.where(pos, r, er)
    key = one_m_lb * jnp.where(pos, er, r)
    return f, key


_EPILOGUES = {
    "none": lambda y: y,
    "silu": _silu,
    "silu_a": lambda y: _silu(y) * HEAD_A ** -0.5,
    "scale_b": lambda y: y * (HEAD_B ** -0.5 * LOG2E),
}


def _bf16_weight(w_ref, scr):
    if not scr:
        return w_ref

    @pl.when(pl.program_id(0) == 0)
    def _():
        scr[0][...] = w_ref[...].astype(BF16)

    return scr[0]


def _weight_scratch(w):
    return [] if w.dtype == BF16 else [pltpu.VMEM(w.shape[-2:], BF16)]


def _norm_proj_kernel(x_ref, g_ref, w_ref, *rest, widths, epilogues):
    out_refs, scr = rest[:len(widths)], rest[len(widths):]
    w = _bf16_weight(w_ref, scr)
    h = _rms(x_ref[...], g_ref[...]).astype(BF16)
    off = 0
    for o_ref, wd, ep in zip(out_refs, widths, epilogues):
        o_ref[...] = _EPILOGUES[ep](_dot(h, w[:, off:off + wd])).astype(o_ref.dtype)
        off += wd


def _norm_proj(x2d, g, w, widths, epilogues, dtypes, tm, layer=0):
    m, d = x2d.shape
    n = w.shape[-1]
    tm = min(tm, m)
    assert sum(widths) == n and m % tm == 0
    return pl.pallas_call(
        functools.partial(_norm_proj_kernel, widths=tuple(widths), epilogues=tuple(epilogues)),
        grid=(m // tm,),
        in_specs=[
            pl.BlockSpec((tm, d), lambda i: (i, 0)),
            pl.BlockSpec((1, d), lambda i: (0, 0)),
            pl.BlockSpec((None, d, n), lambda i: (layer, 0, 0), pipeline_mode=pl.Buffered(1)),
        ],
        out_specs=[pl.BlockSpec((tm, wd), lambda i: (i, 0)) for wd in widths],
        out_shape=[jax.ShapeDtypeStruct((m, wd), dt) for wd, dt in zip(widths, dtypes)],
        scratch_shapes=_weight_scratch(w),
        compiler_params=pltpu.CompilerParams(
            dimension_semantics=("arbitrary",), vmem_limit_bytes=VMEM_LIMIT),
        name="norm_proj",
    )(x2d, g.reshape(1, d).astype(F32), w)


def _out_proj_kernel(a_ref, w_ref, x_ref, g_ref, w2_ref, *rest, mode):
    n_out = 2 if mode == "project" else 1
    out_refs, scr = rest[:n_out], rest[n_out:]
    y = x_ref[...] + _dot(a_ref[...].astype(BF16), _bf16_weight(w_ref, scr)[...])
    if mode == "norm":
        y = _rms(y, g_ref[...])
    out_refs[0][...] = y
    if mode == "project":
        out_refs[1][...] = _dot(_rms(y, g_ref[...]).astype(BF16), w2_ref[...].astype(BF16))


def _out_proj(a2d, w, x2d, tm, layer, norm=None, project=None):
    m, k = a2d.shape
    d = w.shape[-1]
    tm = min(tm, m)
    mode = "norm" if norm is not None else "project" if project is not None else "plain"
    g, w2 = (norm, None) if mode == "norm" else project if mode == "project" else (jnp.ones((d,), F32), None)
    if w2 is None:
        w2 = jnp.zeros((8, LANES), F32)
    out_specs = [pl.BlockSpec((tm, d), lambda i: (i, 0))]
    out_shape = [jax.ShapeDtypeStruct((m, d), F32)]
    if mode == "project":
        out_specs.append(pl.BlockSpec((tm, w2.shape[1]), lambda i: (i, 0)))
        out_shape.append(jax.ShapeDtypeStruct((m, w2.shape[1]), F32))
    outs = pl.pallas_call(
        functools.partial(_out_proj_kernel, mode=mode),
        grid=(m // tm,),
        in_specs=[
            pl.BlockSpec((tm, k), lambda i: (i, 0)),
            pl.BlockSpec((None, k, d), lambda i: (layer, 0, 0), pipeline_mode=pl.Buffered(1)),
            pl.BlockSpec((tm, d), lambda i: (i, 0)),
            pl.BlockSpec((1, d), lambda i: (0, 0)),
            pl.BlockSpec(w2.shape, lambda i: (0, 0), pipeline_mode=pl.Buffered(1)),
        ],
        out_specs=out_specs,
        out_shape=out_shape,
        scratch_shapes=_weight_scratch(w),
        compiler_params=pltpu.CompilerParams(
            dimension_semantics=("arbitrary",), vmem_limit_bytes=VMEM_LIMIT),
        name="out_proj",
    )(a2d, w, x2d, g.reshape(1, d).astype(F32), w2)
    return outs if mode == "project" else outs[0]


SUBLANES = 8


def _decay_scores(q, k, f, lvl_ref, chunk):
    nt = chunk // SUBLANES
    tiles = lambda x: [x[SUBLANES * i:SUBLANES * (i + 1)] for i in range(nt)]
    cat = lambda xs: jnp.concatenate(xs, axis=0)
    lvl = lambda i: lvl_ref[SUBLANES * i:SUBLANES * (i + 1), :]
    row = lax.broadcasted_iota(jnp.int32, (SUBLANES, HEAD_A), 0)
    qt, kt = tiles(q), tiles(k)
    et, tt = tiles(f), tiles(f)
    gt = [None] * nt
    k16 = k.astype(BF16)
    prod = _dot_nt(q.astype(BF16), k16)
    att = [jnp.where(lvl(i) == 0, prod[SUBLANES * i:SUBLANES * (i + 1)], 0.0) for i in range(nt)]
    level = 1
    for m in (1, 2, 4):
        a = cat([qt[i] * et[i] for i in range(nt)]).astype(BF16)
        b = k16 if m == 1 else cat([kt[i] * gt[i] for i in range(nt)]).astype(BF16)
        prod = _dot_nt(a, b)
        second = (row & m) != 0
        for i in range(nt):
            att[i] = jnp.where(lvl(i) == level, prod[SUBLANES * i:SUBLANES * (i + 1)], att[i])
            x = jnp.where(second, pltpu.roll(tt[i], m, 0), 1.0)
            y = jnp.where(second, 1.0, pltpu.roll(tt[i], SUBLANES - m, 0))
            et[i] = et[i] * x
            gt[i] = y if gt[i] is None else gt[i] * y
            tt[i] = tt[i] * (x * y)
        level += 1
    mt = 1
    while mt < nt:
        is_second = lambda i: (i // mt) % 2 == 1
        a = cat([qt[i] * et[i] if is_second(i) else qt[i] for i in range(nt)]).astype(BF16)
        b = cat([kt[i] if is_second(i) else kt[i] * gt[i] for i in range(nt)]).astype(BF16)
        prod = _dot_nt(a, b)
        new_t = list(tt)
        for j in range(0, nt, 2 * mt):
            t_first, t_second = tt[j], tt[j + mt]
            both = t_first * t_second
            for i in range(j, j + mt):
                gt[i] = gt[i] * t_second
                new_t[i] = both
            for i in range(j + mt, j + 2 * mt):
                att[i] = jnp.where(lvl(i) == level, prod[SUBLANES * i:SUBLANES * (i + 1)], att[i])
                et[i] = et[i] * t_first
                new_t[i] = both
        tt = new_t
        mt *= 2
        level += 1
    return cat(att), cat(et), cat(gt), tt[0][0:1, :]


def _hgrn_prompt_kernel(q_ref, fz_ref, v_ref, gs_ref, lbp_ref, gon_ref, lvl_ref, *rest,
                        chunk, n_chunks, heads):
    o_ref, s_ref, st_scr, o_scr = rest[-4:]
    tb = pl.program_id(2)

    @pl.when(tb == 0)
    def _():
        st_scr[...] = jnp.zeros_like(st_scr)

    gon = gon_ref[...]
    o_scr[...] = jnp.zeros_like(o_scr)

    def finish(c):
        rows = pl.ds(pl.multiple_of(c * chunk, chunk), chunk)
        for h in range(heads):
            cols = slice(h * HEAD_A, (h + 1) * HEAD_A)
            y = _rms(o_scr[h], gon) * gs_ref[0, rows, cols].astype(F32)
            o_ref[0, rows, cols] = y.astype(o_ref.dtype)

    def body(c, carry):
        finish(jnp.maximum(c - 1, 0))
        rows = pl.ds(pl.multiple_of(c * chunk, chunk), chunk)
        for h in range(heads):
            cols = slice(h * HEAD_A, (h + 1) * HEAD_A)
            q = q_ref[0, rows, cols].astype(F32)
            v = v_ref[0, rows, cols]
            f, k = _gates(fz_ref[0, rows, cols], lbp_ref[0:1, cols], lbp_ref[1:2, cols])
            att, e, g, tot = _decay_scores(q, k, f, lvl_ref, chunk)
            st = st_scr[h]
            o_scr[h] = _dot_nt((q * e).astype(BF16), st.astype(BF16)) + _dot(att.astype(BF16), v)
            st_scr[h] = tot * st + _dot_tn(v, (k * g).astype(BF16))
        return carry

    lax.fori_loop(0, n_chunks, body, 0)
    finish(n_chunks - 1)

    @pl.when(tb == pl.num_programs(2) - 1)
    def _():
        for h in range(heads):
            s_ref[0, 0, h] = st_scr[h].T


def _level_table(chunk):
    t = lax.broadcasted_iota(jnp.int32, (chunk, chunk), 0)
    s = lax.broadcasted_iota(jnp.int32, (chunk, chunk), 1)
    x = t ^ s
    lvl = jnp.zeros((chunk, chunk), jnp.int32)
    m, level = 1, 1
    while m < chunk:
        lvl = jnp.where((x >= m) & (x < 2 * m), level, lvl)
        m *= 2
        level += 1
    return jnp.where(t >= s, lvl, -1)


def _hgrn_prompt(q, fz, v, gs, lbp, gon, states, layer, n_layers, tblk, chunk, heads):
    b, l, di = q.shape
    n_heads = di // HEAD_A
    wd = heads * HEAD_A
    col = lambda i, h, t: (i, t, h)
    in_specs = [
        pl.BlockSpec((1, tblk, wd), col),
        pl.BlockSpec((1, tblk, wd), col),
        pl.BlockSpec((1, tblk, wd), col),
        pl.BlockSpec((1, tblk, wd), col),
        pl.BlockSpec((2, wd), lambda i, h, t: (0, h)),
        pl.BlockSpec((1, HEAD_A), lambda i, h, t: (0, 0)),
        pl.BlockSpec((chunk, chunk), lambda i, h, t: (0, 0)),
    ]
    args = [q, fz, v, gs, lbp, gon.reshape(1, HEAD_A).astype(F32), _level_table(chunk)]
    aliases = {}
    if states is not None:
        in_specs.append(pl.BlockSpec(memory_space=pl.ANY))
        args.append(states)
        aliases = {len(args) - 1: 1}
    return pl.pallas_call(
        functools.partial(_hgrn_prompt_kernel, chunk=chunk, n_chunks=tblk // chunk, heads=heads),
        grid=(b, n_heads // heads, l // tblk),
        in_specs=in_specs,
        out_specs=[
            pl.BlockSpec((1, tblk, wd), col),
            pl.BlockSpec((1, 1, heads, HEAD_A, HEAD_A), lambda i, h, t: (layer, i, h, 0, 0)),
        ],
        out_shape=[
            jax.ShapeDtypeStruct((b, l, di), BF16),
            jax.ShapeDtypeStruct((n_layers, b, n_heads, HEAD_A, HEAD_A), F32),
        ],
        scratch_shapes=[pltpu.VMEM((heads, HEAD_A, HEAD_A), F32), pltpu.VMEM((heads, chunk, HEAD_A), F32)],
        input_output_aliases=aliases,
        compiler_params=pltpu.CompilerParams(
            dimension_semantics=("arbitrary", "arbitrary", "arbitrary"),
            vmem_limit_bytes=VMEM_LIMIT),
        name="hgrn_prompt",
    )(*args)


def _hgrn_sample_kernel(q_ref, fz_ref, v_ref, gs_ref, s0_ref, lbp_ref, gon_ref, *rest, seqs, steps):
    o_ref, s_ref = rest[-2:]
    gon = gon_ref[...]
    row = lax.broadcasted_iota(jnp.int32, (seqs, steps, HEAD_A), 1)
    prow = lax.broadcasted_iota(jnp.int32, (HEAD_A, HEAD_A), 0)
    pad = jnp.zeros((HEAD_A - steps, HEAD_A), F32)
    roll = lambda x, n: pltpu.roll(x, n, 1)

    q = q_ref[...]
    v = v_ref[...]
    f, k = _gates(fz_ref[...], lbp_ref[0:1, :], lbp_ref[1:2, :])
    o = jnp.sum(q * k, axis=-1, keepdims=True) * v
    w = f
    for j in range(1, steps):
        p = jnp.where(row >= j, q * roll(k, j) * w, 0.0)
        o = o + jnp.sum(p, axis=-1, keepdims=True) * roll(v, j)
        w = w * roll(f, j)
    e = f
    g = jnp.where(row < steps - 1, roll(f, steps - 1), 1.0)
    sh = 1
    while sh < steps:
        e = e * jnp.where(row >= sh, roll(e, sh), 1.0)
        g = g * jnp.where(row < steps - sh, roll(g, steps - sh), 1.0)
        sh *= 2
    qe = (q * e).astype(BF16)
    kg = k * g
    outs = []
    for i in range(seqs):
        s0 = s0_ref[0, i, 0]
        outs.append(o[i] + _dot(qe[i], s0.astype(BF16)))
        z = jnp.where(prow == steps, e[i, steps - 1:steps, :], jnp.concatenate([kg[i], pad], 0))
        zt = z.T
        vp = jnp.concatenate([v[i], pad], 0)
        s_ref[0, i, 0] = zt[:, steps:steps + 1] * s0 + _dot(zt.astype(BF16), vp.astype(BF16))
    o_ref[...] = (_rms(jnp.stack(outs), gon) * gs_ref[...]).astype(o_ref.dtype)


def _hgrn_sample(q, fz, v, gs, state_in, lbp, gon, states, layer, seqs):
    b, steps, di = q.shape
    heads = di // HEAD_A
    col = lambda i, h: (i, 0, h)
    st = lambda i, h: (layer, i, h, 0, 0)
    in_specs = [
        pl.BlockSpec((seqs, steps, HEAD_A), col),
        pl.BlockSpec((seqs, steps, HEAD_A), col),
        pl.BlockSpec((seqs, steps, HEAD_A), col),
        pl.BlockSpec((seqs, steps, HEAD_A), col),
        pl.BlockSpec((1, seqs, 1, HEAD_A, HEAD_A), st),
        pl.BlockSpec((2, HEAD_A), lambda i, h: (0, h)),
        pl.BlockSpec((1, HEAD_A), lambda i, h: (0, 0)),
    ]
    args = [q, fz, v, gs, state_in, lbp, gon.reshape(1, HEAD_A).astype(F32)]
    aliases = {}
    if states is not None:
        in_specs.append(pl.BlockSpec(memory_space=pl.ANY))
        args.append(states)
        aliases = {len(args) - 1: 1}
    return pl.pallas_call(
        functools.partial(_hgrn_sample_kernel, seqs=seqs, steps=steps),
        grid=(b // seqs, heads),
        in_specs=in_specs,
        out_specs=[
            pl.BlockSpec((seqs, steps, HEAD_A), col),
            pl.BlockSpec((1, seqs, 1, HEAD_A, HEAD_A), st),
        ],
        out_shape=[
            jax.ShapeDtypeStruct((b, steps, di), BF16),
            jax.ShapeDtypeStruct(state_in.shape, F32),
        ],
        input_output_aliases=aliases,
        compiler_params=pltpu.CompilerParams(
            dimension_semantics=("arbitrary", "arbitrary"), vmem_limit_bytes=VMEM_LIMIT),
        name="hgrn_sample",
    )(*args)


def _swa_prompt_kernel(sink_ref, q_ref, kvc_ref, kvp_ref, gs_ref, o_ref, *, kv_heads):
    nb = pl.program_id(1)
    kv2 = jnp.concatenate([kvp_ref[0], kvc_ref[0]], axis=0).astype(BF16)
    kw = kv_heads * HEAD_B
    lane_k = lax.broadcasted_iota(jnp.int32, (2 * WINDOW, LANES), 1)
    lane_q = lax.broadcasted_iota(jnp.int32, (WINDOW, LANES), 1)
    t_i = lax.broadcasted_iota(jnp.int32, (WINDOW, 2 * WINDOW), 0)
    s_i = lax.broadcasted_iota(jnp.int32, (WINDOW, 2 * WINDOW), 1)
    d = t_i + WINDOW - s_i
    valid = (d >= 0) & (d <= WINDOW) & ((s_i >= WINDOW) | (nb > 0))
    zero = jnp.zeros((), BF16)
    pairs_per_group = GROUP_B // 2
    for j in range(kv_heads // 2):
        kk = kv2[:, j * LANES:(j + 1) * LANES]
        vv = kv2[:, kw + j * LANES:kw + (j + 1) * LANES]
        kk_sw = pltpu.roll(kk.astype(F32), HEAD_B, 1).astype(BF16)
        vv_sw = pltpu.roll(vv.astype(F32), HEAD_B, 1).astype(BF16)
        for par in range(2):
            g = 2 * j + par
            if par == 0:
                k2 = jnp.where(lane_k < HEAD_B, kk, kk_sw)
                v_lo = jnp.where(lane_k < HEAD_B, vv, zero)
                v_hi = jnp.where(lane_k >= HEAD_B, vv_sw, zero)
            else:
                k2 = jnp.where(lane_k >= HEAD_B, kk, kk_sw)
                v_lo = jnp.where(lane_k < HEAD_B, vv_sw, zero)
                v_hi = jnp.where(lane_k >= HEAD_B, vv, zero)
            qs, sinks = [], []
            for half in range(2):
                for pr in range(pairs_per_group):
                    pidx = g * pairs_per_group + pr
                    qp = q_ref[0, :, pidx * LANES:(pidx + 1) * LANES]
                    qs.append(jnp.where((lane_q >= HEAD_B) == (half == 1), qp, zero))
                    sinks.append(jnp.full((WINDOW, LANES), sink_ref[2 * pidx + half] * LOG2E, F32))
            sink = jnp.stack(sinks)
            s = _dot_nt(jnp.concatenate(qs, axis=0), k2).reshape(GROUP_B, WINDOW, 2 * WINDOW)
            s = jnp.where(valid[None], s, -jnp.inf)
            mx = jnp.broadcast_to(jnp.max(s, axis=-1, keepdims=True), sink.shape)
            p = jnp.exp2(s - jnp.concatenate([mx, mx], axis=-1))
            den = jnp.sum(p, axis=-1, keepdims=True) + jnp.exp2(sink - mx)
            p = p.astype(BF16).reshape(GROUP_B * WINDOW, 2 * WINDOW)
            rden = (1.0 / den).reshape(2, pairs_per_group * WINDOW, LANES)
            half_rows = pairs_per_group * WINDOW
            o = (_dot(p[:half_rows], v_lo) * rden[0] + _dot(p[half_rows:], v_hi) * rden[1])
            for pr in range(pairs_per_group):
                pidx = g * pairs_per_group + pr
                acc = o[pr * WINDOW:(pr + 1) * WINDOW]
                gs = gs_ref[0, :, pidx * LANES:(pidx + 1) * LANES].astype(F32)
                o_ref[0, :, pidx * LANES:(pidx + 1) * LANES] = (acc * gs).astype(o_ref.dtype)


def _swa_prompt(q, kv, gs, sinks):
    b, l, di = q.shape
    kvw = kv.shape[-1]
    kv_heads = kvw // (2 * HEAD_B)
    return pl.pallas_call(
        functools.partial(_swa_prompt_kernel, kv_heads=kv_heads),
        grid=(b, l // WINDOW),
        in_specs=[
            pl.BlockSpec(memory_space=pltpu.SMEM),
            pl.BlockSpec((1, WINDOW, di), lambda i, n: (i, n, 0)),
            pl.BlockSpec((1, WINDOW, kvw), lambda i, n: (i, n, 0)),
            pl.BlockSpec((1, WINDOW, kvw), lambda i, n: (i, jnp.maximum(n - 1, 0), 0)),
            pl.BlockSpec((1, WINDOW, di), lambda i, n: (i, n, 0)),
        ],
        out_specs=pl.BlockSpec((1, WINDOW, di), lambda i, n: (i, n, 0)),
        out_shape=jax.ShapeDtypeStruct((b, l, di), BF16),
        compiler_params=pltpu.CompilerParams(
            dimension_semantics=("arbitrary", "arbitrary"), vmem_limit_bytes=VMEM_LIMIT),
        name="swa_prompt",
    )(sinks.astype(F32), q, kv, kv, gs)


def _swa_sample_kernel(sink_ref, q_ref, kvn_ref, cache_ref, gs_ref, o_ref, *, seqs, steps, kv_heads):
    kw = kv_heads * HEAD_B
    pairs = GROUP_B // 2
    rows = GROUP_B * steps
    half_rows = pairs * steps
    lane_q = lax.broadcasted_iota(jnp.int32, (steps, LANES), 1)
    lane_k = lax.broadcasted_iota(jnp.int32, (WINDOW, LANES), 1)
    t_i = lax.broadcasted_iota(jnp.int32, (rows, WINDOW), 0) % steps
    c_i = lax.broadcasted_iota(jnp.int32, (rows, WINDOW), 1)
    valid_c = c_i >= t_i
    valid_n = c_i <= t_i
    pad = jnp.zeros((WINDOW - steps, LANES), F32)
    blocks = [(i, g) for i in range(seqs) for g in range(kv_heads)]

    def kv_tiles(i, g, off):
        col = slice(off + (g // 2) * LANES, off + (g // 2 + 1) * LANES)
        xs = (cache_ref[i, :, col], jnp.concatenate([kvn_ref[i, :, col], pad], 0))
        return [(x, pltpu.roll(x, HEAD_B, 1)) for x in xs]

    scores, sinks = [], []
    for i, g in blocks:
        own_low = g % 2 == 0
        qs = []
        for half in range(2):
            for pr in range(pairs):
                pidx = g * pairs + pr
                tile = q_ref[i, :, pidx * LANES:(pidx + 1) * LANES]
                qs.append(jnp.where((lane_q >= HEAD_B) == (half == 1), tile, 0.0))
                sinks.append(jnp.full((steps, LANES), sink_ref[2 * pidx + half] * LOG2E, F32))
        q = jnp.concatenate(qs, axis=0).astype(BF16)
        kc, kn = [jnp.where((lane_k < HEAD_B) == own_low, x, sw).astype(BF16) for x, sw in kv_tiles(i, g, 0)]
        sc = jnp.where(valid_c, _dot_nt(q, kc), -jnp.inf)
        sn = jnp.where(valid_n, _dot_nt(q, kn), -jnp.inf)
        scores.append(jnp.concatenate([sc, sn], axis=1))
    s = jnp.concatenate(scores, axis=0)
    sink = jnp.concatenate(sinks, axis=0)
    mx = jnp.broadcast_to(jnp.max(s, axis=-1, keepdims=True), sink.shape)
    p = jnp.exp2(s - jnp.concatenate([mx, mx], axis=1))
    rden = 1.0 / (jnp.sum(p, axis=-1, keepdims=True) + jnp.exp2(sink - mx))
    p = p.astype(BF16)
    for n, (i, g) in enumerate(blocks):
        own_low = g % 2 == 0
        o = None
        for half in range(2):
            r = slice(n * rows + half * half_rows, n * rows + (half + 1) * half_rows)
            vc, vn = [jnp.where((lane_k >= HEAD_B) == (half == 1), x if own_low == (half == 0) else sw, 0.0)
                      .astype(BF16) for x, sw in kv_tiles(i, g, kw)]
            oh = (_dot(p[r, :WINDOW], vc) + _dot(p[r, WINDOW:], vn)) * rden[r]
            o = oh if o is None else o + oh
        for pr in range(pairs):
            cols = slice((g * pairs + pr) * LANES, (g * pairs + pr + 1) * LANES)
            o_ref[i, :, cols] = o[pr * steps:(pr + 1) * steps] * gs_ref[i, :, cols]


def _swa_sample(q, kv_new, cache, gs, sinks, seqs):
    b, steps, di = q.shape
    kvw = kv_new.shape[-1]
    kv_heads = kvw // (2 * HEAD_B)
    return pl.pallas_call(
        functools.partial(_swa_sample_kernel, seqs=seqs, steps=steps, kv_heads=kv_heads),
        grid=(b // seqs,),
        in_specs=[
            pl.BlockSpec(memory_space=pltpu.SMEM),
            pl.BlockSpec((seqs, steps, di), lambda i: (i, 0, 0)),
            pl.BlockSpec((seqs, steps, kvw), lambda i: (i, 0, 0)),
            pl.BlockSpec((seqs, WINDOW, kvw), lambda i: (i, 0, 0)),
            pl.BlockSpec((seqs, steps, di), lambda i: (i, 0, 0)),
        ],
        out_specs=pl.BlockSpec((seqs, steps, di), lambda i: (i, 0, 0)),
        out_shape=jax.ShapeDtypeStruct((b, steps, di), F32),
        compiler_params=pltpu.CompilerParams(
            dimension_semantics=("arbitrary",), vmem_limit_bytes=VMEM_LIMIT),
        name="swa_sample",
    )(sinks.astype(F32), q, kv_new, cache, gs)


def _lower_bound_params(p):
    lb = jnp.cumsum(jax.nn.softmax(p.astype(F32), axis=0), axis=0)
    lb = lb - lb[:1]
    return jnp.stack([lb, 1.0 - lb], axis=1)


def kernel(x_prompt, x_sample, state_hgrn, cache_kv_window, w_in_a, w_out_a, norm_a, onorm_a,
           lower_bounds_a, norm_kv, w_kv, w_in_b, w_out_b, norm_b, sinks_b, norm_f):
    bp, lp, dm = x_prompt.shape
    bs, ls, _ = x_sample.shape
    n_a = w_in_a.shape[0]
    n_b = w_in_b.shape[0]
    di = w_out_a.shape[1]
    kvw = w_kv.shape[1]
    kv_heads = kvw // (2 * HEAD_B)
    wb = cache_kv_window.shape[1]
    assert wb == WINDOW and lp % WINDOW == 0 and ls <= WINDOW

    lbp = _lower_bound_params(lower_bounds_a)
    hp = x_prompt.reshape(bp * lp, dm)
    hs = x_sample.reshape(bs * ls, dm)
    cache = cache_kv_window.reshape(bs, wb, kvw)

    state_prompt = state_sample = None
    kv_p = kv_s = None
    w_in_a16 = w_in_a.astype(BF16)
    for layer in range(n_a):
        project = (norm_kv, w_kv) if layer == n_a - 1 else None
        splits = dict(widths=(di,) * 4, epilogues=("silu_a", "none", "none", "silu"))
        parts = _norm_proj(hp, norm_a[layer], w_in_a16, dtypes=(BF16, F32, BF16, BF16), tm=256, layer=layer, **splits)
        og, state_prompt = _hgrn_prompt(
            *(a.reshape(bp, lp, di) for a in parts), lbp[layer], onorm_a[layer],
            state_prompt, layer, n_a, tblk=min(1024, lp), chunk=min(256, lp), heads=4)
        hp = _out_proj(og.reshape(bp * lp, di), w_out_a, hp, tm=512, layer=layer, project=project)

        parts = _norm_proj(hs, norm_a[layer], w_in_a16, dtypes=(F32,) * 4, tm=256, layer=layer, **splits)
        og, state_sample = _hgrn_sample(
            *(a.reshape(bs, ls, di) for a in parts), state_hgrn, lbp[layer], onorm_a[layer],
            state_sample, layer, seqs=min(16, bs))
        hs = _out_proj(og.reshape(bs * ls, di), w_out_a, hs, tm=512, layer=layer, project=project)
    hp, kv_p = hp
    hs, kv_s = hs
    kv_p = kv_p.reshape(bp, lp, kvw)
    kv_s = kv_s.reshape(bs, ls, kvw)

    for j in range(n_b):
        g_final = norm_f if j == n_b - 1 else None
        splits = dict(widths=(di, di), epilogues=("scale_b", "silu"))
        q, gs = _norm_proj(hp, norm_b[j], w_in_b, dtypes=(BF16, BF16), tm=512, layer=j, **splits)
        o = _swa_prompt(q.reshape(bp, lp, di), kv_p, gs.reshape(bp, lp, di), sinks_b[j])
        hp = _out_proj(o.reshape(bp * lp, di), w_out_b, hp, tm=512, layer=j, norm=g_final)

        q, gs = _norm_proj(hs, norm_b[j], w_in_b, dtypes=(F32, F32), tm=512, layer=j, **splits)
        o = _swa_sample(q.reshape(bs, ls, di), kv_s, cache, gs.reshape(bs, ls, di), sinks_b[j], seqs=8)
        hs = _out_proj(o.reshape(bs * ls, di), w_out_b, hs, tm=512, norm=g_final, layer=j)

    kv_shape = (2, kv_heads, HEAD_B)
    y_prompt = hp.reshape(bp, lp, dm)
    y_sample = hs.reshape(bs, ls, dm)
    kv_window_prompt = kv_p[:, lp - min(WINDOW, lp):].reshape(bp, min(WINDOW, lp), *kv_shape)
    kv_window_sample = jnp.concatenate([cache, kv_s], axis=1)[:, ls:].reshape(bs, wb, *kv_shape)
    return (y_prompt, y_sample, state_prompt, state_sample, kv_window_prompt, kv_window_sample)
```

```python
import functools

import jax
import jax.numpy as jnp
from jax import lax
from jax.experimental import pallas as pl
from jax.experimental.pallas import tpu as pltpu

F32 = jnp.float32
BF16 = jnp.bfloat16

EPS = 1e-6
HEAD_A = 128
HEAD_B = 64
GROUP_B = 8
WINDOW = 128
LANES = 128
VMEM_LIMIT = 56 * 1024 * 1024
LOG2E = 1.4426950408889634


def _dot(a, b):
    return jnp.dot(a, b, preferred_element_type=F32)


def _dot_nt(a, b):
    return lax.dot_general(a, b, (((1,), (1,)), ((), ())), preferred_element_type=F32)


def _dot_tn(a, b):
    return lax.dot_general(a, b, (((0,), (0,)), ((), ())), preferred_element_type=F32)


def _silu(x):
    return x * (0.5 * jnp.tanh(0.5 * x) + 0.5)


def _rms(x, g):
    return x * lax.rsqrt(jnp.mean(x * x, axis=-1, keepdims=True) + EPS) * g


def _gates(fz, lb, one_m_lb):
    e = jnp.exp(-jnp.abs(fz))
    r = 1.0 / (1.0 + e)
    er = e * r
    pos = fz >= 0.0
    f = lb + one_m_lb * jnp.where(pos, r, er)
    key = one_m_lb * jnp.where(pos, er, r)
    return f, key


_EPILOGUES = {
    "none": lambda y: y,
    "silu": _silu,
    "silu_a": lambda y: _silu(y) * HEAD_A ** -0.5,
    "scale_b": lambda y: y * (HEAD_B ** -0.5 * LOG2E),
}


def _bf16_weight(w_ref, scr):
    if not scr:
        return w_ref

    @pl.when(pl.program_id(0) == 0)
    def _():
        scr[0][...] = w_ref[...].astype(BF16)

    return scr[0]


def _weight_scratch(w):
    return [] if w.dtype == BF16 else [pltpu.VMEM(w.shape[-2:], BF16)]


def _norm_proj_kernel(x_ref, g_ref, w_ref, *rest, widths, epilogues):
    out_refs, scr = rest[:len(widths)], rest[len(widths):]
    w = _bf16_weight(w_ref, scr)
    h = _rms(x_ref[...], g_ref[...]).astype(BF16)
    off = 0
    for o_ref, wd, ep in zip(out_refs, widths, epilogues):
        o_ref[...] = _EPILOGUES[ep](_dot(h, w[:, off:off + wd])).astype(o_ref.dtype)
        off += wd


def _norm_proj(x2d, g, w, widths, epilogues, dtypes, tm, layer=0):
    m, d = x2d.shape
    n = w.shape[-1]
    tm = min(tm, m)
    assert sum(widths) == n and m % tm == 0
    return pl.pallas_call(
        functools.partial(_norm_proj_kernel, widths=tuple(widths), epilogues=tuple(epilogues)),
        grid=(m // tm,),
        in_specs=[
            pl.BlockSpec((tm, d), lambda i: (i, 0)),
            pl.BlockSpec((1, d), lambda i: (0, 0)),
            pl.BlockSpec((None, d, n), lambda i: (layer, 0, 0), pipeline_mode=pl.Buffered(1)),
        ],
        out_specs=[pl.BlockSpec((tm, wd), lambda i: (i, 0)) for wd in widths],
        out_shape=[jax.ShapeDtypeStruct((m, wd), dt) for wd, dt in zip(widths, dtypes)],
        scratch_shapes=_weight_scratch(w),
        compiler_params=pltpu.CompilerParams(
            dimension_semantics=("arbitrary",), vmem_limit_bytes=VMEM_LIMIT),
        name="norm_proj",
    )(x2d, g.reshape(1, d).astype(F32), w)


def _out_proj_kernel(a_ref, w_ref, x_ref, g_ref, w2_ref, *rest, mode):
    n_out = 2 if mode == "project" else 1
    out_refs, scr = rest[:n_out], rest[n_out:]
    y = x_ref[...] + _dot(a_ref[...].astype(BF16), _bf16_weight(w_ref, scr)[...])
    if mode == "norm":
        y = _rms(y, g_ref[...])
    out_refs[0][...] = y
    if mode == "project":
        out_refs[1][...] = _dot(_rms(y, g_ref[...]).astype(BF16), w2_ref[...].astype(BF16))


def _out_proj(a2d, w, x2d, tm, layer, norm=None, project=None):
    m, k = a2d.shape
    d = w.shape[-1]
    tm = min(tm, m)
    mode = "norm" if norm is not None else "project" if project is not None else "plain"
    g, w2 = (norm, None) if mode == "norm" else project if mode == "project" else (jnp.ones((d,), F32), None)
    if w2 is None:
        w2 = jnp.zeros((8, LANES), F32)
    out_specs = [pl.BlockSpec((tm, d), lambda i: (i, 0))]
    out_shape = [jax.ShapeDtypeStruct((m, d), F32)]
    if mode == "project":
        out_specs.append(pl.BlockSpec((tm, w2.shape[1]), lambda i: (i, 0)))
        out_shape.append(jax.ShapeDtypeStruct((m, w2.shape[1]), F32))
    outs = pl.pallas_call(
        functools.partial(_out_proj_kernel, mode=mode),
        grid=(m // tm,),
        in_specs=[
            pl.BlockSpec((tm, k), lambda i: (i, 0)),
            pl.BlockSpec((None, k, d), lambda i: (layer, 0, 0), pipeline_mode=pl.Buffered(1)),
            pl.BlockSpec((tm, d), lambda i: (i, 0)),
            pl.BlockSpec((1, d), lambda i: (0, 0)),
            pl.BlockSpec(w2.shape, lambda i: (0, 0), pipeline_mode=pl.Buffered(1)),
        ],
        out_specs=out_specs,
        out_shape=out_shape,
        scratch_shapes=_weight_scratch(w),
        compiler_params=pltpu.CompilerParams(
            dimension_semantics=("arbitrary",), vmem_limit_bytes=VMEM_LIMIT),
        name="out_proj",
    )(a2d, w, x2d, g.reshape(1, d).astype(F32), w2)
    return outs if mode == "project" else outs[0]


SUBLANES = 8


def _decay_scores(q, k, f, lvl_ref, chunk):
    nt = chunk // SUBLANES
    tiles = lambda x: [x[SUBLANES * i:SUBLANES * (i + 1)] for i in range(nt)]
    cat = lambda xs: jnp.concatenate(xs, axis=0)
    row = lax.broadcasted_iota(jnp.int32, (SUBLANES, HEAD_A), 0)
    qt, kt = tiles(q), tiles(k)
    et, tt = tiles(f), tiles(f)
    gt = [None] * nt
    att = [[None] * (chunk // LANES) for _ in range(nt)]

    def take(prod, level, i, lo, hi, p=None):
        r = slice(SUBLANES * i, SUBLANES * (i + 1))
        rp = r if p is None else slice(SUBLANES * p, SUBLANES * (p + 1))
        for j in range(lo // LANES, (hi - 1) // LANES + 1):
            c = slice(LANES * j, LANES * (j + 1))
            keep = 0.0 if att[i][j] is None else att[i][j]
            att[i][j] = jnp.where(lvl_ref[r, c] == level, prod[rp, c], keep)

    k16 = k.astype(BF16)
    prod = _dot_nt(q.astype(BF16), k16)
    for i in range(nt):
        take(prod, 0, i, SUBLANES * i, SUBLANES * (i + 1))
    level = 1
    for m in (1, 2, 4):
        a = cat([qt[i] * et[i] for i in range(nt)]).astype(BF16)
        b = k16 if m == 1 else cat([kt[i] * gt[i] for i in range(nt)]).astype(BF16)
        prod = _dot_nt(a, b)
        second = (row & m) != 0
        for i in range(nt):
            take(prod, level, i, SUBLANES * i, SUBLANES * (i + 1))
            x = jnp.where(second, pltpu.roll(tt[i], m, 0), 1.0)
            y = jnp.where(second, 1.0, pltpu.roll(tt[i], SUBLANES - m, 0))
            et[i] = et[i] * x
            gt[i] = y if gt[i] is None else gt[i] * y
            tt[i] = tt[i] * (x * y)
        level += 1
    mt = 1
    while mt < nt:
        is_second = lambda i: (i // mt) % 2 == 1
        second_tiles = [i for i in range(nt) if is_second(i)]
        a = cat([qt[i] * et[i] for i in second_tiles]).astype(BF16)
        b = cat([kt[i] if is_second(i) else kt[i] * gt[i] for i in range(nt)]).astype(BF16)
        prod = _dot_nt(a, b)
        new_t = list(tt)
        for j in range(0, nt, 2 * mt):
            t_first, t_second = tt[j], tt[j + mt]
            both = t_first * t_second
            for i in range(j, j + mt):
                gt[i] = gt[i] * t_second
                new_t[i] = both
            for i in range(j + mt, j + 2 * mt):
                take(prod, level, i, SUBLANES * j, SUBLANES * (j + mt), second_tiles.index(i))
                et[i] = et[i] * t_first
                new_t[i] = both
        tt = new_t
        mt *= 2
        level += 1
    zero = jnp.zeros((SUBLANES, LANES), F32)
    att = cat([jnp.concatenate([zero if p is None else p for p in pieces], axis=1) for pieces in att])
    return att, cat(et), cat(gt), tt[0][0:1, :]


def _hgrn_prompt_kernel(q_ref, fz_ref, v_ref, gs_ref, lbp_ref, gon_ref, lvl_ref, *rest,
                        chunk, n_chunks, heads, slab):
    o_ref, s_ref, st_scr, o_scr = rest[-4:]
    tb = pl.program_id(2)

    @pl.when(tb == 0)
    def _():
        st_scr[...] = jnp.zeros_like(st_scr)

    gon = gon_ref[...]
    o_scr[...] = jnp.zeros_like(o_scr)

    def finish(c):
        rows = pl.ds(pl.multiple_of(c * chunk, chunk), chunk)
        for h in range(heads):
            cols = slice(h * HEAD_A, (h + 1) * HEAD_A)
            y = _rms(o_scr[h], gon) * gs_ref[0, rows, cols].astype(F32)
            o_ref[0, rows, cols] = y.astype(o_ref.dtype)

    def body(c, carry):
        finish(jnp.maximum(c - 1, 0))
        rows = pl.ds(pl.multiple_of(c * chunk, chunk), chunk)
        for h in range(heads):
            cols = slice(h * HEAD_A, (h + 1) * HEAD_A)
            q = q_ref[0, rows, cols].astype(F32)
            v = v_ref[0, rows, cols]
            f, k = _gates(fz_ref[0, rows, cols], lbp_ref[0:1, cols], lbp_ref[1:2, cols])
            att, e, g, tot = _decay_scores(q, k, f, lvl_ref, chunk)
            st = st_scr[h]
            o_scr[h] = _dot_nt((q * e).astype(BF16), st.astype(BF16)) + _dot(att.astype(BF16), v)
            st_scr[h] = tot * st + _dot_tn(v, (k * g).astype(BF16))
        return carry

    lax.fori_loop(0, n_chunks, body, 0)
    finish(n_chunks - 1)

    @pl.when(tb == pl.num_programs(2) - 1)
    def _():
        s_ref[...] = jnp.zeros_like(s_ref)
        for h in range(heads):
            s_ref[slab, 0, h] = st_scr[h].T


def _level_table(chunk):
    t = lax.broadcasted_iota(jnp.int32, (chunk, chunk), 0)
    s = lax.broadcasted_iota(jnp.int32, (chunk, chunk), 1)
    x = t ^ s
    lvl = jnp.zeros((chunk, chunk), jnp.int32)
    m, level = 1, 1
    while m < chunk:
        lvl = jnp.where((x >= m) & (x < 2 * m), level, lvl)
        m *= 2
        level += 1
    return jnp.where(t >= s, lvl, -1)


def _hgrn_prompt(q, fz, v, gs, lbp, gon, states, layer, n_layers, tblk, chunk, heads):
    b, l, di = q.shape
    n_heads = di // HEAD_A
    wd = heads * HEAD_A
    col = lambda i, h, t: (i, t, h)
    in_specs = [
        pl.BlockSpec((1, tblk, wd), col),
        pl.BlockSpec((1, tblk, wd), col),
        pl.BlockSpec((1, tblk, wd), col),
        pl.BlockSpec((1, tblk, wd), col),
        pl.BlockSpec((2, wd), lambda i, h, t: (0, h)),
        pl.BlockSpec((1, HEAD_A), lambda i, h, t: (0, 0)),
        pl.BlockSpec((chunk, chunk), lambda i, h, t: (0, 0)),
    ]
    args = [q, fz, v, gs, lbp, gon.reshape(1, HEAD_A).astype(F32), _level_table(chunk)]
    aliases = {}
    if states is not None:
        in_specs.append(pl.BlockSpec(memory_space=pl.ANY))
        args.append(states)
        aliases = {len(args) - 1: 1}
    return pl.pallas_call(
        functools.partial(_hgrn_prompt_kernel, chunk=chunk, n_chunks=tblk // chunk, heads=heads,
                          slab=layer if states is None else 0),
        grid=(b, n_heads // heads, l // tblk),
        in_specs=in_specs,
        out_specs=[
            pl.BlockSpec((1, tblk, wd), col),
            (pl.BlockSpec((n_layers, 1, heads, HEAD_A, HEAD_A), lambda i, h, t: (0, i, h, 0, 0))
             if states is None else
             pl.BlockSpec((1, 1, heads, HEAD_A, HEAD_A), lambda i, h, t: (layer, i, h, 0, 0))),
        ],
        out_shape=[
            jax.ShapeDtypeStruct((b, l, di), BF16),
            jax.ShapeDtypeStruct((n_layers, b, n_heads, HEAD_A, HEAD_A), F32),
        ],
        scratch_shapes=[pltpu.VMEM((heads, HEAD_A, HEAD_A), F32), pltpu.VMEM((heads, chunk, HEAD_A), F32)],
        input_output_aliases=aliases,
        compiler_params=pltpu.CompilerParams(
            dimension_semantics=("arbitrary", "arbitrary", "arbitrary"),
            vmem_limit_bytes=VMEM_LIMIT),
        name="hgrn_prompt",
    )(*args)


def _hgrn_sample_kernel(q_ref, fz_ref, v_ref, gs_ref, s0_ref, lbp_ref, gon_ref, *rest, seqs, steps, slab,
                        fill):
    o_ref, s_ref = rest[-2:]
    gon = gon_ref[...]
    row = lax.broadcasted_iota(jnp.int32, (seqs, steps, HEAD_A), 1)
    prow = lax.broadcasted_iota(jnp.int32, (HEAD_A, HEAD_A), 0)
    pad = jnp.zeros((HEAD_A - steps, HEAD_A), F32)
    roll = lambda x, n: pltpu.roll(x, n, 1)

    q = q_ref[...]
    v = v_ref[...]
    f, k = _gates(fz_ref[...], lbp_ref[0:1, :], lbp_ref[1:2, :])
    o = jnp.sum(q * k, axis=-1, keepdims=True) * v
    w = f
    for j in range(1, steps):
        p = jnp.where(row >= j, q * roll(k, j) * w, 0.0)
        o = o + jnp.sum(p, axis=-1, keepdims=True) * roll(v, j)
        w = w * roll(f, j)
    e = f
    g = jnp.where(row < steps - 1, roll(f, steps - 1), 1.0)
    sh = 1
    while sh < steps:
        e = e * jnp.where(row >= sh, roll(e, sh), 1.0)
        g = g * jnp.where(row < steps - sh, roll(g, steps - sh), 1.0)
        sh *= 2
    qe = (q * e).astype(BF16)
    kg = k * g
    if fill:
        s_ref[...] = jnp.zeros_like(s_ref)
    outs = []
    for i in range(seqs):
        s0 = s0_ref[0, i, 0]
        outs.append(o[i] + _dot(qe[i], s0.astype(BF16)))
        z = jnp.where(prow == steps, e[i, steps - 1:steps, :], jnp.concatenate([kg[i], pad], 0))
        zt = z.T
        vp = jnp.concatenate([v[i], pad], 0)
        s_ref[slab, i, 0] = zt[:, steps:steps + 1] * s0 + _dot(zt.astype(BF16), vp.astype(BF16))
    o_ref[...] = (_rms(jnp.stack(outs), gon) * gs_ref[...]).astype(o_ref.dtype)


def _hgrn_sample(q, fz, v, gs, state_in, lbp, gon, states, layer, seqs):
    b, steps, di = q.shape
    heads = di // HEAD_A
    col = lambda i, h: (i, 0, h)
    st = lambda i, h: (layer, i, h, 0, 0)
    in_specs = [
        pl.BlockSpec((seqs, steps, HEAD_A), col),
        pl.BlockSpec((seqs, steps, HEAD_A), col),
        pl.BlockSpec((seqs, steps, HEAD_A), col),
        pl.BlockSpec((seqs, steps, HEAD_A), col),
        pl.BlockSpec((1, seqs, 1, HEAD_A, HEAD_A), st),
        pl.BlockSpec((2, HEAD_A), lambda i, h: (0, h)),
        pl.BlockSpec((1, HEAD_A), lambda i, h: (0, 0)),
    ]
    args = [q, fz, v, gs, state_in, lbp, gon.reshape(1, HEAD_A).astype(F32)]
    aliases = {}
    if states is not None:
        in_specs.append(pl.BlockSpec(memory_space=pl.ANY))
        args.append(states)
        aliases = {len(args) - 1: 1}
    return pl.pallas_call(
        functools.partial(_hgrn_sample_kernel, seqs=seqs, steps=steps, slab=layer if states is None else 0,
                          fill=states is None),
        grid=(b // seqs, heads),
        in_specs=in_specs,
        out_specs=[
            pl.BlockSpec((seqs, steps, HEAD_A), col),
            (pl.BlockSpec((state_in.shape[0], seqs, 1, HEAD_A, HEAD_A), lambda i, h: (0, i, h, 0, 0))
             if states is None else pl.BlockSpec((1, seqs, 1, HEAD_A, HEAD_A), st)),
        ],
        out_shape=[
            jax.ShapeDtypeStruct((b, steps, di), BF16),
            jax.ShapeDtypeStruct(state_in.shape, F32),
        ],
        input_output_aliases=aliases,
        compiler_params=pltpu.CompilerParams(
            dimension_semantics=("arbitrary", "arbitrary"), vmem_limit_bytes=VMEM_LIMIT),
        name="hgrn_sample",
    )(*args)


def _swa_prompt_kernel(sink_ref, q_ref, kvc_ref, kvp_ref, gs_ref, o_ref, *, kv_heads):
    nb = pl.program_id(1)
    kv2 = jnp.concatenate([kvp_ref[0], kvc_ref[0]], axis=0).astype(BF16)
    kw = kv_heads * HEAD_B
    lane_k = lax.broadcasted_iota(jnp.int32, (2 * WINDOW, LANES), 1)
    lane_q = lax.broadcasted_iota(jnp.int32, (WINDOW, LANES), 1)
    t_i = lax.broadcasted_iota(jnp.int32, (WINDOW, 2 * WINDOW), 0)
    s_i = lax.broadcasted_iota(jnp.int32, (WINDOW, 2 * WINDOW), 1)
    d = t_i + WINDOW - s_i
    valid = (d >= 0) & (d <= WINDOW) & ((s_i >= WINDOW) | (nb > 0))
    zero = jnp.zeros((), BF16)
    pairs_per_group = GROUP_B // 2
    for j in range(kv_heads // 2):
        kk = kv2[:, j * LANES:(j + 1) * LANES]
        vv = kv2[:, kw + j * LANES:kw + (j + 1) * LANES]
        kk_sw = pltpu.roll(kk.astype(F32), HEAD_B, 1).astype(BF16)
        vv_sw = pltpu.roll(vv.astype(F32), HEAD_B, 1).astype(BF16)
        for par in range(2):
            g = 2 * j + par
            if par == 0:
                k2 = jnp.where(lane_k < HEAD_B, kk, kk_sw)
                v_lo = jnp.where(lane_k < HEAD_B, vv, zero)
                v_hi = jnp.where(lane_k >= HEAD_B, vv_sw, zero)
            else:
                k2 = jnp.where(lane_k >= HEAD_B, kk, kk_sw)
                v_lo = jnp.where(lane_k < HEAD_B, vv_sw, zero)
                v_hi = jnp.where(lane_k >= HEAD_B, vv, zero)
            qs, sinks = [], []
            for half in range(2):
                for pr in range(pairs_per_group):
                    pidx = g * pairs_per_group + pr
                    qp = q_ref[0, :, pidx * LANES:(pidx + 1) * LANES]
                    qs.append(jnp.where((lane_q >= HEAD_B) == (half == 1), qp, zero))
                    sinks.append(jnp.full((WINDOW, LANES), sink_ref[2 * pidx + half] * LOG2E, F32))
            sink = jnp.stack(sinks)
            s = _dot_nt(jnp.concatenate(qs, axis=0), k2).reshape(GROUP_B, WINDOW, 2 * WINDOW)
            s = jnp.where(valid[None], s, -jnp.inf)
            mx = jnp.broadcast_to(jnp.max(s, axis=-1, keepdims=True), sink.shape)
            p = jnp.exp2(s - jnp.concatenate([mx, mx], axis=-1))
            den = jnp.sum(p, axis=-1, keepdims=True) + jnp.exp2(sink - mx)
            p = p.astype(BF16).reshape(GROUP_B * WINDOW, 2 * WINDOW)
            rden = (1.0 / den).reshape(2, pairs_per_group * WINDOW, LANES)
            half_rows = pairs_per_group * WINDOW
            o = (_dot(p[:half_rows], v_lo) * rden[0] + _dot(p[half_rows:], v_hi) * rden[1])
            for pr in range(pairs_per_group):
                pidx = g * pairs_per_group + pr
                acc = o[pr * WINDOW:(pr + 1) * WINDOW]
                gs = gs_ref[0, :, pidx * LANES:(pidx + 1) * LANES].astype(F32)
                o_ref[0, :, pidx * LANES:(pidx + 1) * LANES] = (acc * gs).astype(o_ref.dtype)


def _swa_prompt(q, kv, gs, sinks):
    b, l, di = q.shape
    kvw = kv.shape[-1]
    kv_heads = kvw // (2 * HEAD_B)
    return pl.pallas_call(
        functools.partial(_swa_prompt_kernel, kv_heads=kv_heads),
        grid=(b, l // WINDOW),
        in_specs=[
            pl.BlockSpec(memory_space=pltpu.SMEM),
            pl.BlockSpec((1, WINDOW, di), lambda i, n: (i, n, 0)),
            pl.BlockSpec((1, WINDOW, kvw), lambda i, n: (i, n, 0)),
            pl.BlockSpec((1, WINDOW, kvw), lambda i, n: (i, jnp.maximum(n - 1, 0), 0)),
            pl.BlockSpec((1, WINDOW, di), lambda i, n: (i, n, 0)),
        ],
        out_specs=pl.BlockSpec((1, WINDOW, di), lambda i, n: (i, n, 0)),
        out_shape=jax.ShapeDtypeStruct((b, l, di), BF16),
        compiler_params=pltpu.CompilerParams(
            dimension_semantics=("arbitrary", "arbitrary"), vmem_limit_bytes=VMEM_LIMIT),
        name="swa_prompt",
    )(sinks.astype(F32), q, kv, kv, gs)


def _swa_sample_kernel(sink_ref, q_ref, kvn_ref, cache_ref, gs_ref, o_ref, *, seqs, steps, kv_heads):
    kw = kv_heads * HEAD_B
    pairs = GROUP_B // 2
    rows = GROUP_B * steps
    half_rows = pairs * steps
    lane_q = lax.broadcasted_iota(jnp.int32, (steps, LANES), 1)
    lane_k = lax.broadcasted_iota(jnp.int32, (WINDOW, LANES), 1)
    t_i = lax.broadcasted_iota(jnp.int32, (rows, WINDOW), 0) % steps
    c_i = lax.broadcasted_iota(jnp.int32, (rows, WINDOW), 1)
    valid_c = c_i >= t_i
    valid_n = c_i <= t_i
    pad = jnp.zeros((WINDOW - steps, LANES), F32)
    blocks = [(i, g) for i in range(seqs) for g in range(kv_heads)]

    def kv_tiles(i, g, off):
        col = slice(off + (g // 2) * LANES, off + (g // 2 + 1) * LANES)
        xs = (cache_ref[i, :, col], jnp.concatenate([kvn_ref[i, :, col], pad], 0))
        return [(x, pltpu.roll(x, HEAD_B, 1)) for x in xs]

    scores, sinks = [], []
    for i, g in blocks:
        own_low = g % 2 == 0
        qs = []
        for half in range(2):
            for pr in range(pairs):
                pidx = g * pairs + pr
                tile = q_ref[i, :, pidx * LANES:(pidx + 1) * LANES]
                qs.append(jnp.where((lane_q >= HEAD_B) == (half == 1), tile, 0.0))
                sinks.append(jnp.full((steps, LANES), sink_ref[2 * pidx + half] * LOG2E, F32))
        q = jnp.concatenate(qs, axis=0).astype(BF16)
        kc, kn = [jnp.where((lane_k < HEAD_B) == own_low, x, sw).astype(BF16) for x, sw in kv_tiles(i, g, 0)]
        sc = jnp.where(valid_c, _dot_nt(q, kc), -jnp.inf)
        sn = jnp.where(valid_n, _dot_nt(q, kn), -jnp.inf)
        scores.append(jnp.concatenate([sc, sn], axis=1))
    s = jnp.concatenate(scores, axis=0)
    sink = jnp.concatenate(sinks, axis=0)
    mx = jnp.broadcast_to(jnp.max(s, axis=-1, keepdims=True), sink.shape)
    p = jnp.exp2(s - jnp.concatenate([mx, mx], axis=1))
    rden = 1.0 / (jnp.sum(p, axis=-1, keepdims=True) + jnp.exp2(sink - mx))
    p = p.astype(BF16)
    for n, (i, g) in enumerate(blocks):
        own_low = g % 2 == 0
        o = None
        for half in range(2):
            r = slice(n * rows + half * half_rows, n * rows + (half + 1) * half_rows)
            vc, vn = [jnp.where((lane_k >= HEAD_B) == (half == 1), x if own_low == (half == 0) else sw, 0.0)
                      .astype(BF16) for x, sw in kv_tiles(i, g, kw)]
            oh = (_dot(p[r, :WINDOW], vc) + _dot(p[r, WINDOW:], vn)) * rden[r]
            o = oh if o is None else o + oh
        for pr in range(pairs):
            cols = slice((g * pairs + pr) * LANES, (g * pairs + pr + 1) * LANES)
            o_ref[i, :, cols] = o[pr * steps:(pr + 1) * steps] * gs_ref[i, :, cols]


def _swa_sample(q, kv_new, cache, gs, sinks, seqs):
    b, steps, di = q.shape
    kvw = kv_new.shape[-1]
    kv_heads = kvw // (2 * HEAD_B)
    return pl.pallas_call(
        functools.partial(_swa_sample_kernel, seqs=seqs, steps=steps, kv_heads=kv_heads),
        grid=(b // seqs,),
        in_specs=[
            pl.BlockSpec(memory_space=pltpu.SMEM),
            pl.BlockSpec((seqs, steps, di), lambda i: (i, 0, 0)),
            pl.BlockSpec((seqs, steps, kvw), lambda i: (i, 0, 0)),
            pl.BlockSpec((seqs, WINDOW, kvw), lambda i: (i, 0, 0)),
            pl.BlockSpec((seqs, steps, di), lambda i: (i, 0, 0)),
        ],
        out_specs=pl.BlockSpec((seqs, steps, di), lambda i: (i, 0, 0)),
        out_shape=jax.ShapeDtypeStruct((b, steps, di), F32),
        compiler_params=pltpu.CompilerParams(
            dimension_semantics=("arbitrary",), vmem_limit_bytes=VMEM_LIMIT),
        name="swa_sample",
    )(sinks.astype(F32), q, kv_new, cache, gs)


def _lower_bound_params(p):
    lb = jnp.cumsum(jax.nn.softmax(p.astype(F32), axis=0), axis=0)
    lb = lb - lb[:1]
    return jnp.stack([lb, 1.0 - lb], axis=1)


def kernel(x_prompt, x_sample, state_hgrn, cache_kv_window, w_in_a, w_out_a, norm_a, onorm_a,
           lower_bounds_a, norm_kv, w_kv, w_in_b, w_out_b, norm_b, sinks_b, norm_f):
    bp, lp, dm = x_prompt.shape
    bs, ls, _ = x_sample.shape
    n_a = w_in_a.shape[0]
    n_b = w_in_b.shape[0]
    di = w_out_a.shape[1]
    kvw = w_kv.shape[1]
    kv_heads = kvw // (2 * HEAD_B)
    wb = cache_kv_window.shape[1]
    assert wb == WINDOW and lp % WINDOW == 0 and ls <= WINDOW

    lbp = _lower_bound_params(lower_bounds_a)
    hp = x_prompt.reshape(bp * lp, dm)
    hs = x_sample.reshape(bs * ls, dm)
    cache = cache_kv_window.reshape(bs, wb, kvw)

    state_prompt = state_sample = None
    kv_p = kv_s = None
    w_in_a16 = w_in_a.astype(BF16)
    for layer in range(n_a):
        project = (norm_kv, w_kv) if layer == n_a - 1 else None
        splits = dict(widths=(di,) * 4, epilogues=("silu_a", "none", "none", "silu"))
        parts = _norm_proj(hp, norm_a[layer], w_in_a16, dtypes=(BF16, F32, BF16, BF16), tm=256, layer=layer, **splits)
        og, state_prompt = _hgrn_prompt(
            *(a.reshape(bp, lp, di) for a in parts), lbp[layer], onorm_a[layer],
            state_prompt, layer, n_a, tblk=min(1024, lp), chunk=min(256, lp), heads=4)
        hp = _out_proj(og.reshape(bp * lp, di), w_out_a, hp, tm=512, layer=layer, project=project)

        parts = _norm_proj(hs, norm_a[layer], w_in_a16, dtypes=(F32,) * 4, tm=256, layer=layer, **splits)
        og, state_sample = _hgrn_sample(
            *(a.reshape(bs, ls, di) for a in parts), state_hgrn, lbp[layer], onorm_a[layer],
            state_sample, layer, seqs=min(16, bs))
        hs = _out_proj(og.reshape(bs * ls, di), w_out_a, hs, tm=512, layer=layer, project=project)
    hp, kv_p = hp
    hs, kv_s = hs
    kv_p = kv_p.reshape(bp, lp, kvw)
    kv_s = kv_s.reshape(bs, ls, kvw)

    for j in range(n_b):
        g_final = norm_f if j == n_b - 1 else None
        splits = dict(widths=(di, di), epilogues=("scale_b", "silu"))
        q, gs = _norm_proj(hp, norm_b[j], w_in_b, dtypes=(BF16, BF16), tm=512, layer=j, **splits)
        o = _swa_prompt(q.reshape(bp, lp, di), kv_p, gs.reshape(bp, lp, di), sinks_b[j])
        hp = _out_proj(o.reshape(bp * lp, di), w_out_b, hp, tm=512, layer=j, norm=g_final)

        q, gs = _norm_proj(hs, norm_b[j], w_in_b, dtypes=(F32, F32), tm=512, layer=j, **splits)
        o = _swa_sample(q.reshape(bs, ls, di), kv_s, cache, gs.reshape(bs, ls, di), sinks_b[j], seqs=8)
        hs = _out_proj(o.reshape(bs * ls, di), w_out_b, hs, tm=512, norm=g_final, layer=j)

    kv_shape = (2, kv_heads, HEAD_B)
    y_prompt = hp.reshape(bp, lp, dm)
    y_sample = hs.reshape(bs, ls, dm)
    kv_window_prompt = kv_p[:, lp - min(WINDOW, lp):].reshape(bp, min(WINDOW, lp), *kv_shape)
    kv_window_sample = jnp.concatenate([cache, kv_s], axis=1)[:, ls:].reshape(bs, wb, *kv_shape)
    return (y_prompt, y_sample, state_prompt, state_sample, kv_window_prompt, kv_window_sample)
```

```python
import functools

import jax
import jax.numpy as jnp
from jax import lax
from jax.experimental import pallas as pl
from jax.experimental.pallas import tpu as pltpu

F32 = jnp.float32
BF16 = jnp.bfloat16

EPS = 1e-6
HEAD_A = 128
HEAD_B = 64
GROUP_B = 8
WINDOW = 128
LANES = 128
VMEM_LIMIT = 56 * 1024 * 1024
LOG2E = 1.4426950408889634
MXU_COLS = 256


def _dot(a, b):
    return jnp.dot(a, b, preferred_element_type=F32)


def _dot_nt(a, b):
    return lax.dot_general(a, b, (((1,), (1,)), ((), ())), preferred_element_type=F32)


def _dot_tn(a, b):
    return lax.dot_general(a, b, (((0,), (0,)), ((), ())), preferred_element_type=F32)


def _silu(x):
    return x * (0.5 * jnp.tanh(0.5 * x) + 0.5)


def _rms(x, g):
    return x * lax.rsqrt(jnp.mean(x * x, axis=-1, keepdims=True) + EPS) * g


def _gates(fz, lb, one_m_lb):
    e = jnp.exp(-jnp.abs(fz))
    r = 1.0 / (1.0 + e)
    er = e * r
    pos = fz >= 0.0
    f = lb + one_m_lb * jnp.where(pos, r, er)
    key = one_m_lb * jnp.where(pos, er, r)
    return f, key


_EPILOGUES = {
    "none": lambda y: y,
    "silu": _silu,
    "silu_a": lambda y: _silu(y) * HEAD_A ** -0.5,
    "scale_b": lambda y: y * (HEAD_B ** -0.5 * LOG2E),
}


def _bf16_weight(w_ref, scr):
    if not scr:
        return w_ref

    @pl.when(pl.program_id(0) == 0)
    def _():
        scr[0][...] = w_ref[...].astype(BF16)

    return scr[0]


def _weight_scratch(w):
    return [] if w.dtype == BF16 else [pltpu.VMEM(w.shape[-2:], BF16)]


def _norm_proj_kernel(x_ref, g_ref, w_ref, *rest, widths, epilogues):
    out_refs, scr = rest[:len(widths)], rest[len(widths):]
    w = _bf16_weight(w_ref, scr)
    h = _rms(x_ref[...], g_ref[...]).astype(BF16)
    off = 0
    for o_ref, wd, ep in zip(out_refs, widths, epilogues):
        o_ref[...] = _EPILOGUES[ep](_dot(h, w[:, off:off + wd])).astype(o_ref.dtype)
        off += wd


def _norm_proj(x2d, g, w, widths, epilogues, dtypes, tm, layer=0):
    m, d = x2d.shape
    n = w.shape[-1]
    tm = min(tm, m)
    assert sum(widths) == n and m % tm == 0
    return pl.pallas_call(
        functools.partial(_norm_proj_kernel, widths=tuple(widths), epilogues=tuple(epilogues)),
        grid=(m // tm,),
        in_specs=[
            pl.BlockSpec((tm, d), lambda i: (i, 0)),
            pl.BlockSpec((1, d), lambda i: (0, 0)),
            pl.BlockSpec((None, d, n), lambda i: (layer, 0, 0), pipeline_mode=pl.Buffered(1)),
        ],
        out_specs=[pl.BlockSpec((tm, wd), lambda i: (i, 0)) for wd in widths],
        out_shape=[jax.ShapeDtypeStruct((m, wd), dt) for wd, dt in zip(widths, dtypes)],
        scratch_shapes=_weight_scratch(w),
        compiler_params=pltpu.CompilerParams(
            dimension_semantics=("arbitrary",), vmem_limit_bytes=VMEM_LIMIT),
        name="norm_proj",
    )(x2d, g.reshape(1, d).astype(F32), w)


def _out_proj_kernel(a_ref, w_ref, x_ref, g_ref, w2_ref, *rest, mode):
    n_out = 2 if mode == "project" else 1
    out_refs, scr = rest[:n_out], rest[n_out:]
    w = _bf16_weight(w_ref, scr)
    a = a_ref[...].astype(BF16)
    y_ref = out_refs[0]
    for c0 in range(0, y_ref.shape[1], MXU_COLS):
        c = slice(c0, c0 + MXU_COLS)
        y_ref[:, c] = x_ref[:, c] + _dot(a, w[:, c])
    if mode == "norm":
        y_ref[...] = _rms(y_ref[...], g_ref[...])
    if mode == "project":
        out_refs[1][...] = _dot(_rms(y_ref[...], g_ref[...]).astype(BF16), w2_ref[...].astype(BF16))


def _out_proj(a2d, w, x2d, tm, layer, norm=None, project=None):
    m, k = a2d.shape
    d = w.shape[-1]
    tm = min(tm, m)
    mode = "norm" if norm is not None else "project" if project is not None else "plain"
    g, w2 = (norm, None) if mode == "norm" else project if mode == "project" else (jnp.ones((d,), F32), None)
    if w2 is None:
        w2 = jnp.zeros((8, LANES), F32)
    out_specs = [pl.BlockSpec((tm, d), lambda i: (i, 0))]
    out_shape = [jax.ShapeDtypeStruct((m, d), F32)]
    if mode == "project":
        out_specs.append(pl.BlockSpec((tm, w2.shape[1]), lambda i: (i, 0)))
        out_shape.append(jax.ShapeDtypeStruct((m, w2.shape[1]), F32))
    outs = pl.pallas_call(
        functools.partial(_out_proj_kernel, mode=mode),
        grid=(m // tm,),
        in_specs=[
            pl.BlockSpec((tm, k), lambda i: (i, 0)),
            pl.BlockSpec((None, k, d), lambda i: (layer, 0, 0), pipeline_mode=pl.Buffered(1)),
            pl.BlockSpec((tm, d), lambda i: (i, 0)),
            pl.BlockSpec((1, d), lambda i: (0, 0)),
            pl.BlockSpec(w2.shape, lambda i: (0, 0), pipeline_mode=pl.Buffered(1)),
        ],
        out_specs=out_specs,
        out_shape=out_shape,
        scratch_shapes=_weight_scratch(w),
        compiler_params=pltpu.CompilerParams(
            dimension_semantics=("arbitrary",), vmem_limit_bytes=VMEM_LIMIT),
        name="out_proj",
    )(a2d, w, x2d, g.reshape(1, d).astype(F32), w2)
    return outs if mode == "project" else outs[0]


SUBLANES = 8


def _decay_scores(q, k, f, lvl_ref, chunk):
    nt = chunk // SUBLANES
    tiles = lambda x: [x[SUBLANES * i:SUBLANES * (i + 1)] for i in range(nt)]
    cat = lambda xs: jnp.concatenate(xs, axis=0)
    row = lax.broadcasted_iota(jnp.int32, (SUBLANES, HEAD_A), 0)
    qt, kt = tiles(q), tiles(k)
    et, tt = tiles(f), tiles(f)
    gt = [None] * nt
    att = [[None] * (chunk // LANES) for _ in range(nt)]

    def take(prod, level, i, lo, hi, p=None):
        r = slice(SUBLANES * i, SUBLANES * (i + 1))
        rp = r if p is None else slice(SUBLANES * p, SUBLANES * (p + 1))
        for j in range(lo // LANES, (hi - 1) // LANES + 1):
            c = slice(LANES * j, LANES * (j + 1))
            keep = 0.0 if att[i][j] is None else att[i][j]
            att[i][j] = jnp.where(lvl_ref[r, c] == level, prod[rp, c], keep)

    k16 = k.astype(BF16)
    prod = _dot_nt(q.astype(BF16), k16)
    for i in range(nt):
        take(prod, 0, i, SUBLANES * i, SUBLANES * (i + 1))
    level = 1
    for m in (1, 2, 4):
        a = cat([qt[i] * et[i] for i in range(nt)]).astype(BF16)
        b = k16 if m == 1 else cat([kt[i] * gt[i] for i in range(nt)]).astype(BF16)
        prod = _dot_nt(a, b)
        second = (row & m) != 0
        for i in range(nt):
            take(prod, level, i, SUBLANES * i, SUBLANES * (i + 1))
            x = jnp.where(second, pltpu.roll(tt[i], m, 0), 1.0)
            y = jnp.where(second, 1.0, pltpu.roll(tt[i], SUBLANES - m, 0))
            et[i] = et[i] * x
            gt[i] = y if gt[i] is None else gt[i] * y
            tt[i] = tt[i] * (x * y)
        level += 1
    mt = 1
    while mt < nt:
        is_second = lambda i: (i // mt) % 2 == 1
        second_tiles = [i for i in range(nt) if is_second(i)]
        a = cat([qt[i] * et[i] for i in second_tiles]).astype(BF16)
        b = cat([kt[i] if is_second(i) else kt[i] * gt[i] for i in range(nt)]).astype(BF16)
        prod = _dot_nt(a, b)
        new_t = list(tt)
        for j in range(0, nt, 2 * mt):
            t_first, t_second = tt[j], tt[j + mt]
            both = t_first * t_second
            for i in range(j, j + mt):
                gt[i] = gt[i] * t_second
                new_t[i] = both
            for i in range(j + mt, j + 2 * mt):
                take(prod, level, i, SUBLANES * j, SUBLANES * (j + mt), second_tiles.index(i))
                et[i] = et[i] * t_first
                new_t[i] = both
        tt = new_t
        mt *= 2
        level += 1
    zero = jnp.zeros((SUBLANES, LANES), F32)
    att = cat([jnp.concatenate([zero if p is None else p for p in pieces], axis=1) for pieces in att])
    return att, cat(et), cat(gt), tt[0][0:1, :]


def _hgrn_prompt_kernel(q_ref, fz_ref, v_ref, gs_ref, lbp_ref, gon_ref, lvl_ref, *rest,
                        chunk, n_chunks, heads, slab):
    o_ref, s_ref, st_scr, o_scr = rest[-4:]
    tb = pl.program_id(2)

    @pl.when(tb == 0)
    def _():
        st_scr[...] = jnp.zeros_like(st_scr)

    gon = gon_ref[...]
    o_scr[...] = jnp.zeros_like(o_scr)

    def finish(c):
        rows = pl.ds(pl.multiple_of(c * chunk, chunk), chunk)
        for h in range(heads):
            cols = slice(h * HEAD_A, (h + 1) * HEAD_A)
            y = _rms(o_scr[h], gon) * gs_ref[0, rows, cols].astype(F32)
            o_ref[0, rows, cols] = y.astype(o_ref.dtype)

    def body(c, carry):
        finish(jnp.maximum(c - 1, 0))
        rows = pl.ds(pl.multiple_of(c * chunk, chunk), chunk)
        for h in range(heads):
            cols = slice(h * HEAD_A, (h + 1) * HEAD_A)
            q = q_ref[0, rows, cols].astype(F32)
            v = v_ref[0, rows, cols]
            f, k = _gates(fz_ref[0, rows, cols], lbp_ref[0:1, cols], lbp_ref[1:2, cols])
            att, e, g, tot = _decay_scores(q, k, f, lvl_ref, chunk)
            st = st_scr[h]
            o_scr[h] = _dot_nt((q * e).astype(BF16), st.astype(BF16)) + _dot(att.astype(BF16), v)
            st_scr[h] = tot * st + _dot_tn(v, (k * g).astype(BF16))
        return carry

    lax.fori_loop(0, n_chunks, body, 0)
    finish(n_chunks - 1)

    @pl.when(tb == pl.num_programs(2) - 1)
    def _():
        s_ref[...] = jnp.zeros_like(s_ref)
        for h in range(heads):
            s_ref[slab, 0, h] = st_scr[h].T


def _level_table(chunk):
    t = lax.broadcasted_iota(jnp.int32, (chunk, chunk), 0)
    s = lax.broadcasted_iota(jnp.int32, (chunk, chunk), 1)
    x = t ^ s
    lvl = jnp.zeros((chunk, chunk), jnp.int32)
    m, level = 1, 1
    while m < chunk:
        lvl = jnp.where((x >= m) & (x < 2 * m), level, lvl)
        m *= 2
        level += 1
    return jnp.where(t >= s, lvl, -1)


def _hgrn_prompt(q, fz, v, gs, lbp, gon, states, layer, n_layers, tblk, chunk, heads):
    b, l, di = q.shape
    n_heads = di // HEAD_A
    wd = heads * HEAD_A
    col = lambda i, h, t: (i, t, h)
    in_specs = [
        pl.BlockSpec((1, tblk, wd), col),
        pl.BlockSpec((1, tblk, wd), col),
        pl.BlockSpec((1, tblk, wd), col),
        pl.BlockSpec((1, tblk, wd), col),
        pl.BlockSpec((2, wd), lambda i, h, t: (0, h)),
        pl.BlockSpec((1, HEAD_A), lambda i, h, t: (0, 0)),
        pl.BlockSpec((chunk, chunk), lambda i, h, t: (0, 0)),
    ]
    args = [q, fz, v, gs, lbp, gon.reshape(1, HEAD_A).astype(F32), _level_table(chunk)]
    aliases = {}
    if states is not None:
        in_specs.append(pl.BlockSpec(memory_space=pl.ANY))
        args.append(states)
        aliases = {len(args) - 1: 1}
    return pl.pallas_call(
        functools.partial(_hgrn_prompt_kernel, chunk=chunk, n_chunks=tblk // chunk, heads=heads,
                          slab=layer if states is None else 0),
        grid=(b, n_heads // heads, l // tblk),
        in_specs=in_specs,
        out_specs=[
            pl.BlockSpec((1, tblk, wd), col),
            (pl.BlockSpec((n_layers, 1, heads, HEAD_A, HEAD_A), lambda i, h, t: (0, i, h, 0, 0))
             if states is None else
             pl.BlockSpec((1, 1, heads, HEAD_A, HEAD_A), lambda i, h, t: (layer, i, h, 0, 0))),
        ],
        out_shape=[
            jax.ShapeDtypeStruct((b, l, di), BF16),
            jax.ShapeDtypeStruct((n_layers, b, n_heads, HEAD_A, HEAD_A), F32),
        ],
        scratch_shapes=[pltpu.VMEM((heads, HEAD_A, HEAD_A), F32), pltpu.VMEM((heads, chunk, HEAD_A), F32)],
        input_output_aliases=aliases,
        compiler_params=pltpu.CompilerParams(
            dimension_semantics=("arbitrary", "arbitrary", "arbitrary"),
            vmem_limit_bytes=VMEM_LIMIT),
        name="hgrn_prompt",
    )(*args)


def _hgrn_sample_kernel(q_ref, fz_ref, v_ref, gs_ref, s0_ref, lbp_ref, gon_ref, *rest, seqs, steps, slab,
                        fill):
    o_ref, s_ref = rest[-2:]
    gon = gon_ref[...]
    row = lax.broadcasted_iota(jnp.int32, (seqs, steps, HEAD_A), 1)
    prow = lax.broadcasted_iota(jnp.int32, (HEAD_A, HEAD_A), 0)
    pad = jnp.zeros((HEAD_A - steps, HEAD_A), F32)
    roll = lambda x, n: pltpu.roll(x, n, 1)

    q = q_ref[...]
    v = v_ref[...]
    f, k = _gates(fz_ref[...], lbp_ref[0:1, :], lbp_ref[1:2, :])
    o = jnp.sum(q * k, axis=-1, keepdims=True) * v
    w = f
    for j in range(1, steps):
        p = jnp.where(row >= j, q * roll(k, j) * w, 0.0)
        o = o + jnp.sum(p, axis=-1, keepdims=True) * roll(v, j)
        w = w * roll(f, j)
    e = f
    g = jnp.where(row < steps - 1, roll(f, steps - 1), 1.0)
    sh = 1
    while sh < steps:
        e = e * jnp.where(row >= sh, roll(e, sh), 1.0)
        g = g * jnp.where(row < steps - sh, roll(g, steps - sh), 1.0)
        sh *= 2
    qe = (q * e).astype(BF16)
    kg = k * g
    if fill:
        s_ref[...] = jnp.zeros_like(s_ref)
    outs = []
    for i in range(seqs):
        s0 = s0_ref[0, i, 0]
        outs.append(o[i] + _dot(qe[i], s0.astype(BF16)))
        z = jnp.where(prow == steps, e[i, steps - 1:steps, :], jnp.concatenate([kg[i], pad], 0))
        zt = z.T
        vp = jnp.concatenate([v[i], pad], 0)
        s_ref[slab, i, 0] = zt[:, steps:steps + 1] * s0 + _dot(zt.astype(BF16), vp.astype(BF16))
    o_ref[...] = (_rms(jnp.stack(outs), gon) * gs_ref[...]).astype(o_ref.dtype)


def _hgrn_sample(q, fz, v, gs, state_in, lbp, gon, states, layer, seqs):
    b, steps, di = q.shape
    heads = di // HEAD_A
    col = lambda i, h: (i, 0, h)
    st = lambda i, h: (layer, i, h, 0, 0)
    in_specs = [
        pl.BlockSpec((seqs, steps, HEAD_A), col),
        pl.BlockSpec((seqs, steps, HEAD_A), col),
        pl.BlockSpec((seqs, steps, HEAD_A), col),
        pl.BlockSpec((seqs, steps, HEAD_A), col),
        pl.BlockSpec((1, seqs, 1, HEAD_A, HEAD_A), st),
        pl.BlockSpec((2, HEAD_A), lambda i, h: (0, h)),
        pl.BlockSpec((1, HEAD_A), lambda i, h: (0, 0)),
    ]
    args = [q, fz, v, gs, state_in, lbp, gon.reshape(1, HEAD_A).astype(F32)]
    aliases = {}
    if states is not None:
        in_specs.append(pl.BlockSpec(memory_space=pl.ANY))
        args.append(states)
        aliases = {len(args) - 1: 1}
    return pl.pallas_call(
        functools.partial(_hgrn_sample_kernel, seqs=seqs, steps=steps, slab=layer if states is None else 0,
                          fill=states is None),
        grid=(b // seqs, heads),
        in_specs=in_specs,
        out_specs=[
            pl.BlockSpec((seqs, steps, HEAD_A), col),
            (pl.BlockSpec((state_in.shape[0], seqs, 1, HEAD_A, HEAD_A), lambda i, h: (0, i, h, 0, 0))
             if states is None else pl.BlockSpec((1, seqs, 1, HEAD_A, HEAD_A), st)),
        ],
        out_shape=[
            jax.ShapeDtypeStruct((b, steps, di), BF16),
            jax.ShapeDtypeStruct(state_in.shape, F32),
        ],
        input_output_aliases=aliases,
        compiler_params=pltpu.CompilerParams(
            dimension_semantics=("arbitrary", "arbitrary"), vmem_limit_bytes=VMEM_LIMIT),
        name="hgrn_sample",
    )(*args)


def _swa_prompt_kernel(sink_ref, q_ref, kvc_ref, kvp_ref, gs_ref, o_ref, *, kv_heads):
    nb = pl.program_id(1)
    kv2 = jnp.concatenate([kvp_ref[0], kvc_ref[0]], axis=0).astype(BF16)
    kw = kv_heads * HEAD_B
    lane_k = lax.broadcasted_iota(jnp.int32, (2 * WINDOW, LANES), 1)
    lane_q = lax.broadcasted_iota(jnp.int32, (WINDOW, LANES), 1)
    t_i = lax.broadcasted_iota(jnp.int32, (WINDOW, 2 * WINDOW), 0)
    s_i = lax.broadcasted_iota(jnp.int32, (WINDOW, 2 * WINDOW), 1)
    d = t_i + WINDOW - s_i
    valid = (d >= 0) & (d <= WINDOW) & ((s_i >= WINDOW) | (nb > 0))
    zero = jnp.zeros((), BF16)
    pairs_per_group = GROUP_B // 2
    for j in range(kv_heads // 2):
        kk = kv2[:, j * LANES:(j + 1) * LANES]
        vv = kv2[:, kw + j * LANES:kw + (j + 1) * LANES]
        kk_sw = pltpu.roll(kk.astype(F32), HEAD_B, 1).astype(BF16)
        vv_sw = pltpu.roll(vv.astype(F32), HEAD_B, 1).astype(BF16)
        for par in range(2):
            g = 2 * j + par
            if par == 0:
                k2 = jnp.where(lane_k < HEAD_B, kk, kk_sw)
                v_lo = jnp.where(lane_k < HEAD_B, vv, zero)
                v_hi = jnp.where(lane_k >= HEAD_B, vv_sw, zero)
            else:
                k2 = jnp.where(lane_k >= HEAD_B, kk, kk_sw)
                v_lo = jnp.where(lane_k < HEAD_B, vv_sw, zero)
                v_hi = jnp.where(lane_k >= HEAD_B, vv, zero)
            qs, sinks = [], []
            for half in range(2):
                for pr in range(pairs_per_group):
                    pidx = g * pairs_per_group + pr
                    qp = q_ref[0, :, pidx * LANES:(pidx + 1) * LANES]
                    qs.append(jnp.where((lane_q >= HEAD_B) == (half == 1), qp, zero))
                    sinks.append(jnp.full((WINDOW, LANES), sink_ref[2 * pidx + half] * LOG2E, F32))
            sink = jnp.stack(sinks)
            s = _dot_nt(jnp.concatenate(qs, axis=0), k2).reshape(GROUP_B, WINDOW, 2 * WINDOW)
            s = jnp.where(valid[None], s, -jnp.inf)
            mx = jnp.broadcast_to(jnp.max(s, axis=-1, keepdims=True), sink.shape)
            p = jnp.exp2(s - jnp.concatenate([mx, mx], axis=-1))
            den = jnp.sum(p, axis=-1, keepdims=True) + jnp.exp2(sink - mx)
            p = p.astype(BF16).reshape(GROUP_B * WINDOW, 2 * WINDOW)
            rden = (1.0 / den).reshape(2, pairs_per_group * WINDOW, LANES)
            half_rows = pairs_per_group * WINDOW
            o = (_dot(p[:half_rows], v_lo) * rden[0] + _dot(p[half_rows:], v_hi) * rden[1])
            for pr in range(pairs_per_group):
                pidx = g * pairs_per_group + pr
                acc = o[pr * WINDOW:(pr + 1) * WINDOW]
                gs = gs_ref[0, :, pidx * LANES:(pidx + 1) * LANES].astype(F32)
                o_ref[0, :, pidx * LANES:(pidx + 1) * LANES] = (acc * gs).astype(o_ref.dtype)


def _swa_prompt(q, kv, gs, sinks):
    b, l, di = q.shape
    kvw = kv.shape[-1]
    kv_heads = kvw // (2 * HEAD_B)
    return pl.pallas_call(
        functools.partial(_swa_prompt_kernel, kv_heads=kv_heads),
        grid=(b, l // WINDOW),
        in_specs=[
            pl.BlockSpec(memory_space=pltpu.SMEM),
            pl.BlockSpec((1, WINDOW, di), lambda i, n: (i, n, 0)),
            pl.BlockSpec((1, WINDOW, kvw), lambda i, n: (i, n, 0)),
            pl.BlockSpec((1, WINDOW, kvw), lambda i, n: (i, jnp.maximum(n - 1, 0), 0)),
            pl.BlockSpec((1, WINDOW, di), lambda i, n: (i, n, 0)),
        ],
        out_specs=pl.BlockSpec((1, WINDOW, di), lambda i, n: (i, n, 0)),
        out_shape=jax.ShapeDtypeStruct((b, l, di), BF16),
        compiler_params=pltpu.CompilerParams(
            dimension_semantics=("arbitrary", "arbitrary"), vmem_limit_bytes=VMEM_LIMIT),
        name="swa_prompt",
    )(sinks.astype(F32), q, kv, kv, gs)


def _swa_sample_kernel(sink_ref, q_ref, kvn_ref, cache_ref, gs_ref, o_ref, *, seqs, steps, kv_heads):
    kw = kv_heads * HEAD_B
    pairs = GROUP_B // 2
    rows = GROUP_B * steps
    half_rows = pairs * steps
    lane_q = lax.broadcasted_iota(jnp.int32, (steps, LANES), 1)
    lane_k = lax.broadcasted_iota(jnp.int32, (WINDOW, LANES), 1)
    t_i = lax.broadcasted_iota(jnp.int32, (rows, WINDOW), 0) % steps
    c_i = lax.broadcasted_iota(jnp.int32, (rows, WINDOW), 1)
    valid_c = c_i >= t_i
    valid_n = c_i <= t_i
    pad = jnp.zeros((WINDOW - steps, LANES), F32)
    blocks = [(i, g) for i in range(seqs) for g in range(kv_heads)]

    def kv_tiles(i, g, off):
        col = slice(off + (g // 2) * LANES, off + (g // 2 + 1) * LANES)
        xs = (cache_ref[i, :, col], jnp.concatenate([kvn_ref[i, :, col], pad], 0))
        return [(x, pltpu.roll(x, HEAD_B, 1)) for x in xs]

    scores, sinks = [], []
    for i, g in blocks:
        own_low = g % 2 == 0
        qs = []
        for half in range(2):
            for pr in range(pairs):
                pidx = g * pairs + pr
                tile = q_ref[i, :, pidx * LANES:(pidx + 1) * LANES]
                qs.append(jnp.where((lane_q >= HEAD_B) == (half == 1), tile, 0.0))
                sinks.append(jnp.full((steps, LANES), sink_ref[2 * pidx + half] * LOG2E, F32))
        q = jnp.concatenate(qs, axis=0).astype(BF16)
        kc, kn = [jnp.where((lane_k < HEAD_B) == own_low, x, sw).astype(BF16) for x, sw in kv_tiles(i, g, 0)]
        sc = jnp.where(valid_c, _dot_nt(q, kc), -jnp.inf)
        sn = jnp.where(valid_n, _dot_nt(q, kn), -jnp.inf)
        scores.append(jnp.concatenate([sc, sn], axis=1))
    s = jnp.concatenate(scores, axis=0)
    sink = jnp.concatenate(sinks, axis=0)
    mx = jnp.broadcast_to(jnp.max(s, axis=-1, keepdims=True), sink.shape)
    p = jnp.exp2(s - jnp.concatenate([mx, mx], axis=1))
    rden = 1.0 / (jnp.sum(p, axis=-1, keepdims=True) + jnp.exp2(sink - mx))
    p = p.astype(BF16)
    for n, (i, g) in enumerate(blocks):
        own_low = g % 2 == 0
        o = None
        for half in range(2):
            r = slice(n * rows + half * half_rows, n * rows + (half + 1) * half_rows)
            vc, vn = [jnp.where((lane_k >= HEAD_B) == (half == 1), x if own_low == (half == 0) else sw, 0.0)
                      .astype(BF16) for x, sw in kv_tiles(i, g, kw)]
            oh = (_dot(p[r, :WINDOW], vc) + _dot(p[r, WINDOW:], vn)) * rden[r]
            o = oh if o is None else o + oh
        for pr in range(pairs):
            cols = slice((g * pairs + pr) * LANES, (g * pairs + pr + 1) * LANES)
            o_ref[i, :, cols] = o[pr * steps:(pr + 1) * steps] * gs_ref[i, :, cols]


def _swa_sample(q, kv_new, cache, gs, sinks, seqs):
    b, steps, di = q.shape
    kvw = kv_new.shape[-1]
    kv_heads = kvw // (2 * HEAD_B)
    return pl.pallas_call(
        functools.partial(_swa_sample_kernel, seqs=seqs, steps=steps, kv_heads=kv_heads),
        grid=(b // seqs,),
        in_specs=[
            pl.BlockSpec(memory_space=pltpu.SMEM),
            pl.BlockSpec((seqs, steps, di), lambda i: (i, 0, 0)),
            pl.BlockSpec((seqs, steps, kvw), lambda i: (i, 0, 0)),
            pl.BlockSpec((seqs, WINDOW, kvw), lambda i: (i, 0, 0)),
            pl.BlockSpec((seqs, steps, di), lambda i: (i, 0, 0)),
        ],
        out_specs=pl.BlockSpec((seqs, steps, di), lambda i: (i, 0, 0)),
        out_shape=jax.ShapeDtypeStruct((b, steps, di), F32),
        compiler_params=pltpu.CompilerParams(
            dimension_semantics=("arbitrary",), vmem_limit_bytes=VMEM_LIMIT),
        name="swa_sample",
    )(sinks.astype(F32), q, kv_new, cache, gs)


def _lower_bound_params(p):
    lb = jnp.cumsum(jax.nn.softmax(p.astype(F32), axis=0), axis=0)
    lb = lb - lb[:1]
    return jnp.stack([lb, 1.0 - lb], axis=1)


def kernel(x_prompt, x_sample, state_hgrn, cache_kv_window, w_in_a, w_out_a, norm_a, onorm_a,
           lower_bounds_a, norm_kv, w_kv, w_in_b, w_out_b, norm_b, sinks_b, norm_f):
    bp, lp, dm = x_prompt.shape
    bs, ls, _ = x_sample.shape
    n_a = w_in_a.shape[0]
    n_b = w_in_b.shape[0]
    di = w_out_a.shape[1]
    kvw = w_kv.shape[1]
    kv_heads = kvw // (2 * HEAD_B)
    wb = cache_kv_window.shape[1]
    assert wb == WINDOW and lp % WINDOW == 0 and ls <= WINDOW

    lbp = _lower_bound_params(lower_bounds_a)
    hp = x_prompt.reshape(bp * lp, dm)
    hs = x_sample.reshape(bs * ls, dm)
    cache = cache_kv_window.reshape(bs, wb, kvw)

    state_prompt = state_sample = None
    kv_p = kv_s = None
    w_in_a16 = w_in_a.astype(BF16)
    for layer in range(n_a):
        project = (norm_kv, w_kv) if layer == n_a - 1 else None
        splits = dict(widths=(di,) * 4, epilogues=("silu_a", "none", "none", "silu"))
        parts = _norm_proj(hp, norm_a[layer], w_in_a16, dtypes=(BF16, F32, BF16, BF16), tm=512, layer=layer, **splits)
        og, state_prompt = _hgrn_prompt(
            *(a.reshape(bp, lp, di) for a in parts), lbp[layer], onorm_a[layer],
            state_prompt, layer, n_a, tblk=min(2048, lp), chunk=min(256, lp), heads=4)
        hp = _out_proj(og.reshape(bp * lp, di), w_out_a, hp, tm=1024, layer=layer, project=project)

        parts = _norm_proj(hs, norm_a[layer], w_in_a16, dtypes=(F32,) * 4, tm=256, layer=layer, **splits)
        og, state_sample = _hgrn_sample(
            *(a.reshape(bs, ls, di) for a in parts), state_hgrn, lbp[layer], onorm_a[layer],
            state_sample, layer, seqs=min(16, bs))
        hs = _out_proj(og.reshape(bs * ls, di), w_out_a, hs, tm=512, layer=layer, project=project)
    hp, kv_p = hp
    hs, kv_s = hs
    kv_p = kv_p.reshape(bp, lp, kvw)
    kv_s = kv_s.reshape(bs, ls, kvw)

    for j in range(n_b):
        g_final = norm_f if j == n_b - 1 else None
        splits = dict(widths=(di, di), epilogues=("scale_b", "silu"))
        q, gs = _norm_proj(hp, norm_b[j], w_in_b, dtypes=(BF16, BF16), tm=512, layer=j, **splits)
        o = _swa_prompt(q.reshape(bp, lp, di), kv_p, gs.reshape(bp, lp, di), sinks_b[j])
        hp = _out_proj(o.reshape(bp * lp, di), w_out_b, hp, tm=1024, layer=j, norm=g_final)

        q, gs = _norm_proj(hs, norm_b[j], w_in_b, dtypes=(F32, F32), tm=512, layer=j, **splits)
        o = _swa_sample(q.reshape(bs, ls, di), kv_s, cache, gs.reshape(bs, ls, di), sinks_b[j], seqs=8)
        hs = _out_proj(o.reshape(bs * ls, di), w_out_b, hs, tm=512, norm=g_final, layer=j)

    kv_shape = (2, kv_heads, HEAD_B)
    y_prompt = hp.reshape(bp, lp, dm)
    y_sample = hs.reshape(bs, ls, dm)
    kv_window_prompt = kv_p[:, lp - min(WINDOW, lp):].reshape(bp, min(WINDOW, lp), *kv_shape)
    kv_window_sample = jnp.concatenate([cache, kv_s], axis=1)[:, ls:].reshape(bs, wb, *kv_shape)
    return (y_prompt, y_sample, state_prompt, state_sample, kv_window_prompt, kv_window_sample)
```

```python
import functools

import jax
import jax.numpy as jnp
from jax import lax
from jax.experimental import pallas as pl
from jax.experimental.pallas import tpu as pltpu

F32 = jnp.float32
BF16 = jnp.bfloat16

EPS = 1e-6
HEAD_A = 128
HEAD_B = 64
GROUP_B = 8
WINDOW = 128
LANES = 128
VMEM_LIMIT = 56 * 1024 * 1024
LOG2E = 1.4426950408889634
MXU_COLS = 256

TILES = dict(
    in_proj_a_rows=512,
    in_proj_b_rows=1024,
    out_proj_rows=1024,
    sample_rows=1024,
    recurrence_tokens=2048,
    recurrence_chunk=256,
    recurrence_heads=4,
    sample_recurrence_seqs=32,
    sample_attention_seqs=16,
)


def _dot(a, b):
    return jnp.dot(a, b, preferred_element_type=F32)


def _dot_nt(a, b):
    return lax.dot_general(a, b, (((1,), (1,)), ((), ())), preferred_element_type=F32)


def _dot_tn(a, b):
    return lax.dot_general(a, b, (((0,), (0,)), ((), ())), preferred_element_type=F32)


def _silu(x):
    return x * (0.5 * jnp.tanh(0.5 * x) + 0.5)


def _rms(x, g):
    return x * lax.rsqrt(jnp.mean(x * x, axis=-1, keepdims=True) + EPS) * g


def _gates(fz, lb, one_m_lb):
    e = jnp.exp(-jnp.abs(fz))
    r = 1.0 / (1.0 + e)
    er = e * r
    pos = fz >= 0.0
    f = lb + one_m_lb * jnp.where(pos, r, er)
    key = one_m_lb * jnp.where(pos, er, r)
    return f, key


_EPILOGUES = {
    "none": lambda y: y,
    "silu": _silu,
    "silu_a": lambda y: _silu(y) * HEAD_A ** -0.5,
    "scale_b": lambda y: y * (HEAD_B ** -0.5 * LOG2E),
}


def _bf16_weight(w_ref, scr):
    if not scr:
        return w_ref

    @pl.when(pl.program_id(0) == 0)
    def _():
        scr[0][...] = w_ref[...].astype(BF16)

    return scr[0]


def _weight_scratch(w):
    return [] if w.dtype == BF16 else [pltpu.VMEM(w.shape[-2:], BF16)]


def _norm_proj_kernel(x_ref, g_ref, w_ref, *rest, widths, epilogues):
    out_refs, scr = rest[:len(widths)], rest[len(widths):]
    w = _bf16_weight(w_ref, scr)
    h = _rms(x_ref[...], g_ref[...]).astype(BF16)
    off = 0
    for o_ref, wd, ep in zip(out_refs, widths, epilogues):
        o_ref[...] = _EPILOGUES[ep](_dot(h, w[:, off:off + wd])).astype(o_ref.dtype)
        off += wd


def _norm_proj(x2d, g, w, widths, epilogues, dtypes, tm, layer=0):
    m, d = x2d.shape
    n = w.shape[-1]
    tm = min(tm, m)
    assert sum(widths) == n and m % tm == 0
    return pl.pallas_call(
        functools.partial(_norm_proj_kernel, widths=tuple(widths), epilogues=tuple(epilogues)),
        grid=(m // tm,),
        in_specs=[
            pl.BlockSpec((tm, d), lambda i: (i, 0)),
            pl.BlockSpec((1, d), lambda i: (0, 0)),
            pl.BlockSpec((None, d, n), lambda i: (layer, 0, 0), pipeline_mode=pl.Buffered(1)),
        ],
        out_specs=[pl.BlockSpec((tm, wd), lambda i: (i, 0)) for wd in widths],
        out_shape=[jax.ShapeDtypeStruct((m, wd), dt) for wd, dt in zip(widths, dtypes)],
        scratch_shapes=_weight_scratch(w),
        compiler_params=pltpu.CompilerParams(
            dimension_semantics=("arbitrary",), vmem_limit_bytes=VMEM_LIMIT),
        name="norm_proj",
    )(x2d, g.reshape(1, d).astype(F32), w)


def _out_proj_kernel(a_ref, w_ref, x_ref, g_ref, w2_ref, *rest, mode):
    n_out = 2 if mode == "project" else 1
    out_refs, scr = rest[:n_out], rest[n_out:]
    w = _bf16_weight(w_ref, scr)
    a = a_ref[...].astype(BF16)
    y_ref = out_refs[0]
    for c0 in range(0, y_ref.shape[1], MXU_COLS):
        c = slice(c0, c0 + MXU_COLS)
        y_ref[:, c] = x_ref[:, c] + _dot(a, w[:, c])
    if mode == "norm":
        y_ref[...] = _rms(y_ref[...], g_ref[...])
    if mode == "project":
        out_refs[1][...] = _dot(_rms(y_ref[...], g_ref[...]).astype(BF16), w2_ref[...].astype(BF16))


def _out_proj(a2d, w, x2d, tm, layer, norm=None, project=None):
    m, k = a2d.shape
    d = w.shape[-1]
    tm = min(tm, m)
    mode = "norm" if norm is not None else "project" if project is not None else "plain"
    g, w2 = (norm, None) if mode == "norm" else project if mode == "project" else (jnp.ones((d,), F32), None)
    if w2 is None:
        w2 = jnp.zeros((8, LANES), F32)
    out_specs = [pl.BlockSpec((tm, d), lambda i: (i, 0))]
    out_shape = [jax.ShapeDtypeStruct((m, d), F32)]
    if mode == "project":
        out_specs.append(pl.BlockSpec((tm, w2.shape[1]), lambda i: (i, 0)))
        out_shape.append(jax.ShapeDtypeStruct((m, w2.shape[1]), F32))
    outs = pl.pallas_call(
        functools.partial(_out_proj_kernel, mode=mode),
        grid=(m // tm,),
        in_specs=[
            pl.BlockSpec((tm, k), lambda i: (i, 0)),
            pl.BlockSpec((None, k, d), lambda i: (layer, 0, 0), pipeline_mode=pl.Buffered(1)),
            pl.BlockSpec((tm, d), lambda i: (i, 0)),
            pl.BlockSpec((1, d), lambda i: (0, 0)),
            pl.BlockSpec(w2.shape, lambda i: (0, 0), pipeline_mode=pl.Buffered(1)),
        ],
        out_specs=out_specs,
        out_shape=out_shape,
        scratch_shapes=_weight_scratch(w),
        compiler_params=pltpu.CompilerParams(
            dimension_semantics=("arbitrary",), vmem_limit_bytes=VMEM_LIMIT),
        name="out_proj",
    )(a2d, w, x2d, g.reshape(1, d).astype(F32), w2)
    return outs if mode == "project" else outs[0]


SUBLANES = 8


def _decay_scores(q, k, f, lvl_ref, chunk):
    nt = chunk // SUBLANES
    tiles = lambda x: [x[SUBLANES * i:SUBLANES * (i + 1)] for i in range(nt)]
    cat = lambda xs: jnp.concatenate(xs, axis=0)
    row = lax.broadcasted_iota(jnp.int32, (SUBLANES, HEAD_A), 0)
    qt, kt = tiles(q), tiles(k)
    et, tt = tiles(f), tiles(f)
    gt = [None] * nt
    att = [[None] * (chunk // LANES) for _ in range(nt)]

    def take(prod, level, i, lo, hi, p=None):
        r = slice(SUBLANES * i, SUBLANES * (i + 1))
        rp = r if p is None else slice(SUBLANES * p, SUBLANES * (p + 1))
        for j in range(lo // LANES, (hi - 1) // LANES + 1):
            c = slice(LANES * j, LANES * (j + 1))
            keep = 0.0 if att[i][j] is None else att[i][j]
            att[i][j] = jnp.where(lvl_ref[r, c] == level, prod[rp, c], keep)

    k16 = k.astype(BF16)
    prod = _dot_nt(q.astype(BF16), k16)
    for i in range(nt):
        take(prod, 0, i, SUBLANES * i, SUBLANES * (i + 1))
    level = 1
    for m in (1, 2, 4):
        a = cat([qt[i] * et[i] for i in range(nt)]).astype(BF16)
        b = k16 if m == 1 else cat([kt[i] * gt[i] for i in range(nt)]).astype(BF16)
        prod = _dot_nt(a, b)
        second = (row & m) != 0
        for i in range(nt):
            take(prod, level, i, SUBLANES * i, SUBLANES * (i + 1))
            x = jnp.where(second, pltpu.roll(tt[i], m, 0), 1.0)
            y = jnp.where(second, 1.0, pltpu.roll(tt[i], SUBLANES - m, 0))
            et[i] = et[i] * x
            gt[i] = y if gt[i] is None else gt[i] * y
            tt[i] = tt[i] * (x * y)
        level += 1
    mt = 1
    while mt < nt:
        is_second = lambda i: (i // mt) % 2 == 1
        second_tiles = [i for i in range(nt) if is_second(i)]
        a = cat([qt[i] * et[i] for i in second_tiles]).astype(BF16)
        b = cat([kt[i] if is_second(i) else kt[i] * gt[i] for i in range(nt)]).astype(BF16)
        prod = _dot_nt(a, b)
        new_t = list(tt)
        for j in range(0, nt, 2 * mt):
            t_first, t_second = tt[j], tt[j + mt]
            both = t_first * t_second
            for i in range(j, j + mt):
                gt[i] = gt[i] * t_second
                new_t[i] = both
            for i in range(j + mt, j + 2 * mt):
                take(prod, level, i, SUBLANES * j, SUBLANES * (j + mt), second_tiles.index(i))
                et[i] = et[i] * t_first
                new_t[i] = both
        tt = new_t
        mt *= 2
        level += 1
    zero = jnp.zeros((SUBLANES, LANES), F32)
    att = cat([jnp.concatenate([zero if p is None else p for p in pieces], axis=1) for pieces in att])
    return att, cat(et), cat(gt), tt[0][0:1, :]


def _hgrn_prompt_kernel(q_ref, fz_ref, v_ref, gs_ref, lbp_ref, gon_ref, lvl_ref, *rest,
                        chunk, n_chunks, heads, slab):
    o_ref, s_ref, st_scr, o_scr = rest[-4:]
    tb = pl.program_id(2)

    @pl.when(tb == 0)
    def _():
        st_scr[...] = jnp.zeros_like(st_scr)

    gon = gon_ref[...]
    o_scr[...] = jnp.zeros_like(o_scr)

    def finish(c):
        rows = pl.ds(pl.multiple_of(c * chunk, chunk), chunk)
        for h in range(heads):
            cols = slice(h * HEAD_A, (h + 1) * HEAD_A)
            y = _rms(o_scr[h], gon) * gs_ref[0, rows, cols].astype(F32)
            o_ref[0, rows, cols] = y.astype(o_ref.dtype)

    def body(c, carry):
        finish(jnp.maximum(c - 1, 0))
        rows = pl.ds(pl.multiple_of(c * chunk, chunk), chunk)
        for h in range(heads):
            cols = slice(h * HEAD_A, (h + 1) * HEAD_A)
            q = q_ref[0, rows, cols].astype(F32)
            v = v_ref[0, rows, cols]
            f, k = _gates(fz_ref[0, rows, cols], lbp_ref[0:1, cols], lbp_ref[1:2, cols])
            att, e, g, tot = _decay_scores(q, k, f, lvl_ref, chunk)
            st = st_scr[h]
            o_scr[h] = _dot_nt((q * e).astype(BF16), st.astype(BF16)) + _dot(att.astype(BF16), v)
            st_scr[h] = tot * st + _dot_tn(v, (k * g).astype(BF16))
        return carry

    lax.fori_loop(0, n_chunks, body, 0)
    finish(n_chunks - 1)

    @pl.when(tb == pl.num_programs(2) - 1)
    def _():
        s_ref[...] = jnp.zeros_like(s_ref)
        for h in range(heads):
            s_ref[slab, 0, h] = st_scr[h].T


def _level_table(chunk):
    t = lax.broadcasted_iota(jnp.int32, (chunk, chunk), 0)
    s = lax.broadcasted_iota(jnp.int32, (chunk, chunk), 1)
    x = t ^ s
    lvl = jnp.zeros((chunk, chunk), jnp.int32)
    m, level = 1, 1
    while m < chunk:
        lvl = jnp.where((x >= m) & (x < 2 * m), level, lvl)
        m *= 2
        level += 1
    return jnp.where(t >= s, lvl, -1)


def _hgrn_prompt(q, fz, v, gs, lbp, gon, states, layer, n_layers, tblk, chunk, heads):
    b, l, di = q.shape
    n_heads = di // HEAD_A
    wd = heads * HEAD_A
    col = lambda i, h, t: (i, t, h)
    in_specs = [
        pl.BlockSpec((1, tblk, wd), col),
        pl.BlockSpec((1, tblk, wd), col),
        pl.BlockSpec((1, tblk, wd), col),
        pl.BlockSpec((1, tblk, wd), col),
        pl.BlockSpec((2, wd), lambda i, h, t: (0, h)),
        pl.BlockSpec((1, HEAD_A), lambda i, h, t: (0, 0)),
        pl.BlockSpec((chunk, chunk), lambda i, h, t: (0, 0)),
    ]
    args = [q, fz, v, gs, lbp, gon.reshape(1, HEAD_A).astype(F32), _level_table(chunk)]
    aliases = {}
    if states is not None:
        in_specs.append(pl.BlockSpec(memory_space=pl.ANY))
        args.append(states)
        aliases = {len(args) - 1: 1}
    return pl.pallas_call(
        functools.partial(_hgrn_prompt_kernel, chunk=chunk, n_chunks=tblk // chunk, heads=heads,
                          slab=layer if states is None else 0),
        grid=(b, n_heads // heads, l // tblk),
        in_specs=in_specs,
        out_specs=[
            pl.BlockSpec((1, tblk, wd), col),
            (pl.BlockSpec((n_layers, 1, heads, HEAD_A, HEAD_A), lambda i, h, t: (0, i, h, 0, 0))
             if states is None else
             pl.BlockSpec((1, 1, heads, HEAD_A, HEAD_A), lambda i, h, t: (layer, i, h, 0, 0))),
        ],
        out_shape=[
            jax.ShapeDtypeStruct((b, l, di), BF16),
            jax.ShapeDtypeStruct((n_layers, b, n_heads, HEAD_A, HEAD_A), F32),
        ],
        scratch_shapes=[pltpu.VMEM((heads, HEAD_A, HEAD_A), F32), pltpu.VMEM((heads, chunk, HEAD_A), F32)],
        input_output_aliases=aliases,
        compiler_params=pltpu.CompilerParams(
            dimension_semantics=("arbitrary", "arbitrary", "arbitrary"),
            vmem_limit_bytes=VMEM_LIMIT),
        name="hgrn_prompt",
    )(*args)


def _hgrn_sample_kernel(q_ref, fz_ref, v_ref, gs_ref, s0_ref, lbp_ref, gon_ref, *rest, seqs, steps, slab,
                        fill):
    o_ref, s_ref = rest[-2:]
    gon = gon_ref[...]
    row = lax.broadcasted_iota(jnp.int32, (seqs, steps, HEAD_A), 1)
    prow = lax.broadcasted_iota(jnp.int32, (HEAD_A, HEAD_A), 0)
    pad = jnp.zeros((HEAD_A - steps, HEAD_A), F32)
    roll = lambda x, n: pltpu.roll(x, n, 1)

    q = q_ref[...]
    v = v_ref[...]
    f, k = _gates(fz_ref[...], lbp_ref[0:1, :], lbp_ref[1:2, :])
    o = jnp.sum(q * k, axis=-1, keepdims=True) * v
    w = f
    for j in range(1, steps):
        p = jnp.where(row >= j, q * roll(k, j) * w, 0.0)
        o = o + jnp.sum(p, axis=-1, keepdims=True) * roll(v, j)
        w = w * roll(f, j)
    e = f
    g = jnp.where(row < steps - 1, roll(f, steps - 1), 1.0)
    sh = 1
    while sh < steps:
        e = e * jnp.where(row >= sh, roll(e, sh), 1.0)
        g = g * jnp.where(row < steps - sh, roll(g, steps - sh), 1.0)
        sh *= 2
    qe = (q * e).astype(BF16)
    kg = k * g
    if fill:
        s_ref[...] = jnp.zeros_like(s_ref)
    outs = []
    for i in range(seqs):
        s0 = s0_ref[0, i, 0]
        outs.append(o[i] + _dot(qe[i], s0.astype(BF16)))
        z = jnp.where(prow == steps, e[i, steps - 1:steps, :], jnp.concatenate([kg[i], pad], 0))
        zt = z.T
        vp = jnp.concatenate([v[i], pad], 0)
        s_ref[slab, i, 0] = zt[:, steps:steps + 1] * s0 + _dot(zt.astype(BF16), vp.astype(BF16))
    o_ref[...] = (_rms(jnp.stack(outs), gon) * gs_ref[...]).astype(o_ref.dtype)


def _hgrn_sample(q, fz, v, gs, state_in, lbp, gon, states, layer, seqs):
    b, steps, di = q.shape
    heads = di // HEAD_A
    col = lambda i, h: (i, 0, h)
    st = lambda i, h: (layer, i, h, 0, 0)
    in_specs = [
        pl.BlockSpec((seqs, steps, HEAD_A), col),
        pl.BlockSpec((seqs, steps, HEAD_A), col),
        pl.BlockSpec((seqs, steps, HEAD_A), col),
        pl.BlockSpec((seqs, steps, HEAD_A), col),
        pl.BlockSpec((1, seqs, 1, HEAD_A, HEAD_A), st),
        pl.BlockSpec((2, HEAD_A), lambda i, h: (0, h)),
        pl.BlockSpec((1, HEAD_A), lambda i, h: (0, 0)),
    ]
    args = [q, fz, v, gs, state_in, lbp, gon.reshape(1, HEAD_A).astype(F32)]
    aliases = {}
    if states is not None:
        in_specs.append(pl.BlockSpec(memory_space=pl.ANY))
        args.append(states)
        aliases = {len(args) - 1: 1}
    return pl.pallas_call(
        functools.partial(_hgrn_sample_kernel, seqs=seqs, steps=steps, slab=layer if states is None else 0,
                          fill=states is None),
        grid=(b // seqs, heads),
        in_specs=in_specs,
        out_specs=[
            pl.BlockSpec((seqs, steps, HEAD_A), col),
            (pl.BlockSpec((state_in.shape[0], seqs, 1, HEAD_A, HEAD_A), lambda i, h: (0, i, h, 0, 0))
             if states is None else pl.BlockSpec((1, seqs, 1, HEAD_A, HEAD_A), st)),
        ],
        out_shape=[
            jax.ShapeDtypeStruct((b, steps, di), BF16),
            jax.ShapeDtypeStruct(state_in.shape, F32),
        ],
        input_output_aliases=aliases,
        compiler_params=pltpu.CompilerParams(
            dimension_semantics=("arbitrary", "arbitrary"), vmem_limit_bytes=VMEM_LIMIT),
        name="hgrn_sample",
    )(*args)


def _swa_prompt_kernel(sink_ref, q_ref, kvc_ref, kvp_ref, gs_ref, o_ref, *, kv_heads):
    nb = pl.program_id(1)
    kv2 = jnp.concatenate([kvp_ref[0], kvc_ref[0]], axis=0).astype(BF16)
    kw = kv_heads * HEAD_B
    lane_k = lax.broadcasted_iota(jnp.int32, (2 * WINDOW, LANES), 1)
    lane_q = lax.broadcasted_iota(jnp.int32, (WINDOW, LANES), 1)
    t_i = lax.broadcasted_iota(jnp.int32, (WINDOW, 2 * WINDOW), 0)
    s_i = lax.broadcasted_iota(jnp.int32, (WINDOW, 2 * WINDOW), 1)
    d = t_i + WINDOW - s_i
    valid = (d >= 0) & (d <= WINDOW) & ((s_i >= WINDOW) | (nb > 0))
    zero = jnp.zeros((), BF16)
    pairs_per_group = GROUP_B // 2
    for j in range(kv_heads // 2):
        kk = kv2[:, j * LANES:(j + 1) * LANES]
        vv = kv2[:, kw + j * LANES:kw + (j + 1) * LANES]
        kk_sw = pltpu.roll(kk.astype(F32), HEAD_B, 1).astype(BF16)
        vv_sw = pltpu.roll(vv.astype(F32), HEAD_B, 1).astype(BF16)
        for par in range(2):
            g = 2 * j + par
            if par == 0:
                k2 = jnp.where(lane_k < HEAD_B, kk, kk_sw)
                v_lo = jnp.where(lane_k < HEAD_B, vv, zero)
                v_hi = jnp.where(lane_k >= HEAD_B, vv_sw, zero)
            else:
                k2 = jnp.where(lane_k >= HEAD_B, kk, kk_sw)
                v_lo = jnp.where(lane_k < HEAD_B, vv_sw, zero)
                v_hi = jnp.where(lane_k >= HEAD_B, vv, zero)
            qs, sinks = [], []
            for half in range(2):
                for pr in range(pairs_per_group):
                    pidx = g * pairs_per_group + pr
                    qp = q_ref[0, :, pidx * LANES:(pidx + 1) * LANES]
                    qs.append(jnp.where((lane_q >= HEAD_B) == (half == 1), qp, zero))
                    sinks.append(jnp.full((WINDOW, LANES), sink_ref[2 * pidx + half] * LOG2E, F32))
            sink = jnp.stack(sinks)
            s = _dot_nt(jnp.concatenate(qs, axis=0), k2).reshape(GROUP_B, WINDOW, 2 * WINDOW)
            s = jnp.where(valid[None], s, -jnp.inf)
            mx = jnp.broadcast_to(jnp.max(s, axis=-1, keepdims=True), sink.shape)
            p = jnp.exp2(s - jnp.concatenate([mx, mx], axis=-1))
            den = jnp.sum(p, axis=-1, keepdims=True) + jnp.exp2(sink - mx)
            p = p.astype(BF16).reshape(GROUP_B * WINDOW, 2 * WINDOW)
            rden = (1.0 / den).reshape(2, pairs_per_group * WINDOW, LANES)
            half_rows = pairs_per_group * WINDOW
            o = (_dot(p[:half_rows], v_lo) * rden[0] + _dot(p[half_rows:], v_hi) * rden[1])
            for pr in range(pairs_per_group):
                pidx = g * pairs_per_group + pr
                acc = o[pr * WINDOW:(pr + 1) * WINDOW]
                gs = gs_ref[0, :, pidx * LANES:(pidx + 1) * LANES].astype(F32)
                o_ref[0, :, pidx * LANES:(pidx + 1) * LANES] = (acc * gs).astype(o_ref.dtype)


def _swa_prompt(q, kv, gs, sinks):
    b, l, di = q.shape
    kvw = kv.shape[-1]
    kv_heads = kvw // (2 * HEAD_B)
    return pl.pallas_call(
        functools.partial(_swa_prompt_kernel, kv_heads=kv_heads),
        grid=(b, l // WINDOW),
        in_specs=[
            pl.BlockSpec(memory_space=pltpu.SMEM),
            pl.BlockSpec((1, WINDOW, di), lambda i, n: (i, n, 0)),
            pl.BlockSpec((1, WINDOW, kvw), lambda i, n: (i, n, 0)),
            pl.BlockSpec((1, WINDOW, kvw), lambda i, n: (i, jnp.maximum(n - 1, 0), 0)),
            pl.BlockSpec((1, WINDOW, di), lambda i, n: (i, n, 0)),
        ],
        out_specs=pl.BlockSpec((1, WINDOW, di), lambda i, n: (i, n, 0)),
        out_shape=jax.ShapeDtypeStruct((b, l, di), BF16),
        compiler_params=pltpu.CompilerParams(
            dimension_semantics=("arbitrary", "arbitrary"), vmem_limit_bytes=VMEM_LIMIT),
        name="swa_prompt",
    )(sinks.astype(F32), q, kv, kv, gs)


def _swa_sample_kernel(sink_ref, q_ref, kvn_ref, cache_ref, gs_ref, o_ref, *, seqs, steps, kv_heads):
    kw = kv_heads * HEAD_B
    pairs = GROUP_B // 2
    rows = GROUP_B * steps
    half_rows = pairs * steps
    lane_q = lax.broadcasted_iota(jnp.int32, (steps, LANES), 1)
    lane_k = lax.broadcasted_iota(jnp.int32, (WINDOW, LANES), 1)
    t_i = lax.broadcasted_iota(jnp.int32, (rows, WINDOW), 0) % steps
    c_i = lax.broadcasted_iota(jnp.int32, (rows, WINDOW), 1)
    valid_c = c_i >= t_i
    valid_n = c_i <= t_i
    pad = jnp.zeros((WINDOW - steps, LANES), F32)
    blocks = [(i, g) for i in range(seqs) for g in range(kv_heads)]

    def kv_tiles(i, g, off):
        col = slice(off + (g // 2) * LANES, off + (g // 2 + 1) * LANES)
        xs = (cache_ref[i, :, col], jnp.concatenate([kvn_ref[i, :, col], pad], 0))
        return [(x, pltpu.roll(x, HEAD_B, 1)) for x in xs]

    scores, sinks = [], []
    for i, g in blocks:
        own_low = g % 2 == 0
        qs = []
        for half in range(2):
            for pr in range(pairs):
                pidx = g * pairs + pr
                tile = q_ref[i, :, pidx * LANES:(pidx + 1) * LANES]
                qs.append(jnp.where((lane_q >= HEAD_B) == (half == 1), tile, 0.0))
                sinks.append(jnp.full((steps, LANES), sink_ref[2 * pidx + half] * LOG2E, F32))
        q = jnp.concatenate(qs, axis=0).astype(BF16)
        kc, kn = [jnp.where((lane_k < HEAD_B) == own_low, x, sw).astype(BF16) for x, sw in kv_tiles(i, g, 0)]
        sc = jnp.where(valid_c, _dot_nt(q, kc), -jnp.inf)
        sn = jnp.where(valid_n, _dot_nt(q, kn), -jnp.inf)
        scores.append(jnp.concatenate([sc, sn], axis=1))
    s = jnp.concatenate(scores, axis=0)
    sink = jnp.concatenate(sinks, axis=0)
    mx = jnp.broadcast_to(jnp.max(s, axis=-1, keepdims=True), sink.shape)
    p = jnp.exp2(s - jnp.concatenate([mx, mx], axis=1))
    rden = 1.0 / (jnp.sum(p, axis=-1, keepdims=True) + jnp.exp2(sink - mx))
    p = p.astype(BF16)
    for n, (i, g) in enumerate(blocks):
        own_low = g % 2 == 0
        o = None
        for half in range(2):
            r = slice(n * rows + half * half_rows, n * rows + (half + 1) * half_rows)
            vc, vn = [jnp.where((lane_k >= HEAD_B) == (half == 1), x if own_low == (half == 0) else sw, 0.0)
                      .astype(BF16) for x, sw in kv_tiles(i, g, kw)]
            oh = (_dot(p[r, :WINDOW], vc) + _dot(p[r, WINDOW:], vn)) * rden[r]
            o = oh if o is None else o + oh
        for pr in range(pairs):
            cols = slice((g * pairs + pr) * LANES, (g * pairs + pr + 1) * LANES)
            o_ref[i, :, cols] = o[pr * steps:(pr + 1) * steps] * gs_ref[i, :, cols]


def _swa_sample(q, kv_new, cache, gs, sinks, seqs):
    b, steps, di = q.shape
    kvw = kv_new.shape[-1]
    kv_heads = kvw // (2 * HEAD_B)
    return pl.pallas_call(
        functools.partial(_swa_sample_kernel, seqs=seqs, steps=steps, kv_heads=kv_heads),
        grid=(b // seqs,),
        in_specs=[
            pl.BlockSpec(memory_space=pltpu.SMEM),
            pl.BlockSpec((seqs, steps, di), lambda i: (i, 0, 0)),
            pl.BlockSpec((seqs, steps, kvw), lambda i: (i, 0, 0)),
            pl.BlockSpec((seqs, WINDOW, kvw), lambda i: (i, 0, 0)),
            pl.BlockSpec((seqs, steps, di), lambda i: (i, 0, 0)),
        ],
        out_specs=pl.BlockSpec((seqs, steps, di), lambda i: (i, 0, 0)),
        out_shape=jax.ShapeDtypeStruct((b, steps, di), F32),
        compiler_params=pltpu.CompilerParams(
            dimension_semantics=("arbitrary",), vmem_limit_bytes=VMEM_LIMIT),
        name="swa_sample",
    )(sinks.astype(F32), q, kv_new, cache, gs)


def _lower_bound_params(p):
    lb = jnp.cumsum(jax.nn.softmax(p.astype(F32), axis=0), axis=0)
    lb = lb - lb[:1]
    return jnp.stack([lb, 1.0 - lb], axis=1)


def kernel(x_prompt, x_sample, state_hgrn, cache_kv_window, w_in_a, w_out_a, norm_a, onorm_a,
           lower_bounds_a, norm_kv, w_kv, w_in_b, w_out_b, norm_b, sinks_b, norm_f):
    bp, lp, dm = x_prompt.shape
    bs, ls, _ = x_sample.shape
    n_a = w_in_a.shape[0]
    n_b = w_in_b.shape[0]
    di = w_out_a.shape[1]
    kvw = w_kv.shape[1]
    kv_heads = kvw // (2 * HEAD_B)
    wb = cache_kv_window.shape[1]
    assert wb == WINDOW and lp % WINDOW == 0 and ls <= WINDOW

    lbp = _lower_bound_params(lower_bounds_a)
    hp = x_prompt.reshape(bp * lp, dm)
    hs = x_sample.reshape(bs * ls, dm)
    cache = cache_kv_window.reshape(bs, wb, kvw)

    state_prompt = state_sample = None
    kv_p = kv_s = None
    w_in_a16 = w_in_a.astype(BF16)
    for layer in range(n_a):
        project = (norm_kv, w_kv) if layer == n_a - 1 else None
        splits = dict(widths=(di,) * 4, epilogues=("silu_a", "none", "none", "silu"))
        parts = _norm_proj(hp, norm_a[layer], w_in_a16, dtypes=(BF16, F32, BF16, BF16), tm=TILES["in_proj_a_rows"],
                           layer=layer, **splits)
        og, state_prompt = _hgrn_prompt(
            *(a.reshape(bp, lp, di) for a in parts), lbp[layer], onorm_a[layer],
            state_prompt, layer, n_a, tblk=min(TILES["recurrence_tokens"], lp),
            chunk=min(TILES["recurrence_chunk"], lp), heads=TILES["recurrence_heads"])
        hp = _out_proj(og.reshape(bp * lp, di), w_out_a, hp, tm=TILES["out_proj_rows"], layer=layer, project=project)

        parts = _norm_proj(hs, norm_a[layer], w_in_a16, dtypes=(F32,) * 4, tm=TILES["sample_rows"], layer=layer, **splits)
        og, state_sample = _hgrn_sample(
            *(a.reshape(bs, ls, di) for a in parts), state_hgrn, lbp[layer], onorm_a[layer],
            state_sample, layer, seqs=min(TILES["sample_recurrence_seqs"], bs))
        hs = _out_proj(og.reshape(bs * ls, di), w_out_a, hs, tm=TILES["sample_rows"], layer=layer, project=project)
    hp, kv_p = hp
    hs, kv_s = hs
    kv_p = kv_p.reshape(bp, lp, kvw)
    kv_s = kv_s.reshape(bs, ls, kvw)

    for j in range(n_b):
        g_final = norm_f if j == n_b - 1 else None
        splits = dict(widths=(di, di), epilogues=("scale_b", "silu"))
        q, gs = _norm_proj(hp, norm_b[j], w_in_b, dtypes=(BF16, BF16), tm=TILES["in_proj_b_rows"], layer=j, **splits)
        o = _swa_prompt(q.reshape(bp, lp, di), kv_p, gs.reshape(bp, lp, di), sinks_b[j])
        hp = _out_proj(o.reshape(bp * lp, di), w_out_b, hp, tm=TILES["out_proj_rows"], layer=j, norm=g_final)

        q, gs = _norm_proj(hs, norm_b[j], w_in_b, dtypes=(F32, F32), tm=TILES["sample_rows"], layer=j, **splits)
        o = _swa_sample(q.reshape(bs, ls, di), kv_s, cache, gs.reshape(bs, ls, di), sinks_b[j],
                        seqs=min(TILES["sample_attention_seqs"], bs))
        hs = _out_proj(o.reshape(bs * ls, di), w_out_b, hs, tm=TILES["sample_rows"], norm=g_final, layer=j)

    kv_shape = (2, kv_heads, HEAD_B)
    y_prompt = hp.reshape(bp, lp, dm)
    y_sample = hs.reshape(bs, ls, dm)
    kv_window_prompt = kv_p[:, lp - min(WINDOW, lp):].reshape(bp, min(WINDOW, lp), *kv_shape)
    kv_window_sample = jnp.concatenate([cache, kv_s], axis=1)[:, ls:].reshape(bs, wb, *kv_shape)
    return (y_prompt, y_sample, state_prompt, state_sample, kv_window_prompt, kv_window_sample)
```

```python
import functools

import jax
import jax.numpy as jnp
from jax import lax
from jax.experimental import pallas as pl
from jax.experimental.pallas import tpu as pltpu

F32 = jnp.float32
BF16 = jnp.bfloat16

EPS = 1e-6
HEAD_A = 128
HEAD_B = 64
GROUP_B = 8
WINDOW = 128
LANES = 128
VMEM_LIMIT = 56 * 1024 * 1024
LOG2E = 1.4426950408889634
MXU_COLS = 256

TILES = dict(
    in_proj_a_rows=512,
    in_proj_b_rows=1024,
    out_proj_rows=1024,
    sample_rows=1024,
    recurrence_tokens=2048,
    recurrence_chunk=256,
    recurrence_heads=4,
    attention_windows=2,
    sample_recurrence_seqs=64,
    sample_attention_seqs=16,
)


def _dot(a, b):
    return jnp.dot(a, b, preferred_element_type=F32)


def _dot_nt(a, b):
    return lax.dot_general(a, b, (((1,), (1,)), ((), ())), preferred_element_type=F32)


def _dot_tn(a, b):
    return lax.dot_general(a, b, (((0,), (0,)), ((), ())), preferred_element_type=F32)


def _silu(x):
    return x * (0.5 * jnp.tanh(0.5 * x) + 0.5)


def _rms(x, g):
    return x * lax.rsqrt(jnp.mean(x * x, axis=-1, keepdims=True) + EPS) * g


def _gates(fz, lb, one_m_lb):
    e = jnp.exp(-jnp.abs(fz))
    r = 1.0 / (1.0 + e)
    er = e * r
    pos = fz >= 0.0
    f = lb + one_m_lb * jnp.where(pos, r, er)
    key = one_m_lb * jnp.where(pos, er, r)
    return f, key


_EPILOGUES = {
    "none": lambda y: y,
    "silu": _silu,
    "silu_a": lambda y: _silu(y) * HEAD_A ** -0.5,
    "scale_b": lambda y: y * (HEAD_B ** -0.5 * LOG2E),
}


def _bf16_weight(w_ref, scr):
    if not scr:
        return w_ref

    @pl.when(pl.program_id(0) == 0)
    def _():
        scr[0][...] = w_ref[...].astype(BF16)

    return scr[0]


def _weight_scratch(w):
    return [] if w.dtype == BF16 else [pltpu.VMEM(w.shape[-2:], BF16)]


def _norm_proj_kernel(x_ref, g_ref, w_ref, *rest, widths, epilogues):
    out_refs, scr = rest[:len(widths)], rest[len(widths):]
    w = _bf16_weight(w_ref, scr)
    h = _rms(x_ref[...], g_ref[...]).astype(BF16)
    off = 0
    for o_ref, wd, ep in zip(out_refs, widths, epilogues):
        o_ref[...] = _EPILOGUES[ep](_dot(h, w[:, off:off + wd])).astype(o_ref.dtype)
        off += wd


def _norm_proj(x2d, g, w, widths, epilogues, dtypes, tm, layer=0):
    m, d = x2d.shape
    n = w.shape[-1]
    tm = min(tm, m)
    assert sum(widths) == n and m % tm == 0
    return pl.pallas_call(
        functools.partial(_norm_proj_kernel, widths=tuple(widths), epilogues=tuple(epilogues)),
        grid=(m // tm,),
        in_specs=[
            pl.BlockSpec((tm, d), lambda i: (i, 0)),
            pl.BlockSpec((1, d), lambda i: (0, 0)),
            pl.BlockSpec((None, d, n), lambda i: (layer, 0, 0), pipeline_mode=pl.Buffered(1)),
        ],
        out_specs=[pl.BlockSpec((tm, wd), lambda i: (i, 0)) for wd in widths],
        out_shape=[jax.ShapeDtypeStruct((m, wd), dt) for wd, dt in zip(widths, dtypes)],
        scratch_shapes=_weight_scratch(w),
        compiler_params=pltpu.CompilerParams(
            dimension_semantics=("arbitrary",), vmem_limit_bytes=VMEM_LIMIT),
        name="norm_proj",
    )(x2d, g.reshape(1, d).astype(F32), w)


def _out_proj_kernel(a_ref, w_ref, x_ref, g_ref, w2_ref, *rest, mode):
    n_out = 2 if mode == "project" else 1
    out_refs, scr = rest[:n_out], rest[n_out:]
    w = _bf16_weight(w_ref, scr)
    a = a_ref[...].astype(BF16)
    y_ref = out_refs[0]
    for c0 in range(0, y_ref.shape[1], MXU_COLS):
        c = slice(c0, c0 + MXU_COLS)
        y_ref[:, c] = x_ref[:, c] + _dot(a, w[:, c])
    if mode == "norm":
        y_ref[...] = _rms(y_ref[...], g_ref[...])
    if mode == "project":
        out_refs[1][...] = _dot(_rms(y_ref[...], g_ref[...]).astype(BF16), w2_ref[...].astype(BF16))


def _out_proj(a2d, w, x2d, tm, layer, norm=None, project=None):
    m, k = a2d.shape
    d = w.shape[-1]
    tm = min(tm, m)
    mode = "norm" if norm is not None else "project" if project is not None else "plain"
    g, w2 = (norm, None) if mode == "norm" else project if mode == "project" else (jnp.ones((d,), F32), None)
    if w2 is None:
        w2 = jnp.zeros((8, LANES), F32)
    out_specs = [pl.BlockSpec((tm, d), lambda i: (i, 0))]
    out_shape = [jax.ShapeDtypeStruct((m, d), F32)]
    if mode == "project":
        out_specs.append(pl.BlockSpec((tm, w2.shape[1]), lambda i: (i, 0)))
        out_shape.append(jax.ShapeDtypeStruct((m, w2.shape[1]), F32))
    outs = pl.pallas_call(
        functools.partial(_out_proj_kernel, mode=mode),
        grid=(m // tm,),
        in_specs=[
            pl.BlockSpec((tm, k), lambda i: (i, 0)),
            pl.BlockSpec((None, k, d), lambda i: (layer, 0, 0), pipeline_mode=pl.Buffered(1)),
            pl.BlockSpec((tm, d), lambda i: (i, 0)),
            pl.BlockSpec((1, d), lambda i: (0, 0)),
            pl.BlockSpec(w2.shape, lambda i: (0, 0), pipeline_mode=pl.Buffered(1)),
        ],
        out_specs=out_specs,
        out_shape=out_shape,
        scratch_shapes=_weight_scratch(w),
        compiler_params=pltpu.CompilerParams(
            dimension_semantics=("arbitrary",), vmem_limit_bytes=VMEM_LIMIT),
        name="out_proj",
    )(a2d, w, x2d, g.reshape(1, d).astype(F32), w2)
    return outs if mode == "project" else outs[0]


SUBLANES = 8


def _decay_scores(q, k, f, lvl_ref, chunk):
    nt = chunk // SUBLANES
    tiles = lambda x: [x[SUBLANES * i:SUBLANES * (i + 1)] for i in range(nt)]
    cat = lambda xs: jnp.concatenate(xs, axis=0)
    row = lax.broadcasted_iota(jnp.int32, (SUBLANES, HEAD_A), 0)
    qt, kt = tiles(q), tiles(k)
    et, tt = tiles(f), tiles(f)
    gt = [None] * nt
    att = [[None] * (chunk // LANES) for _ in range(nt)]

    def take(prod, level, i, lo, hi, p=None):
        r = slice(SUBLANES * i, SUBLANES * (i + 1))
        rp = r if p is None else slice(SUBLANES * p, SUBLANES * (p + 1))
        for j in range(lo // LANES, (hi - 1) // LANES + 1):
            c = slice(LANES * j, LANES * (j + 1))
            keep = 0.0 if att[i][j] is None else att[i][j]
            att[i][j] = jnp.where(lvl_ref[r, c] == level, prod[rp, c], keep)

    k16 = k.astype(BF16)
    prod = _dot_nt(q.astype(BF16), k16)
    for i in range(nt):
        take(prod, 0, i, SUBLANES * i, SUBLANES * (i + 1))
    level = 1
    for m in (1, 2, 4):
        a = cat([qt[i] * et[i] for i in range(nt)]).astype(BF16)
        b = k16 if m == 1 else cat([kt[i] * gt[i] for i in range(nt)]).astype(BF16)
        prod = _dot_nt(a, b)
        second = (row & m) != 0
        for i in range(nt):
            take(prod, level, i, SUBLANES * i, SUBLANES * (i + 1))
            x = jnp.where(second, pltpu.roll(tt[i], m, 0), 1.0)
            y = jnp.where(second, 1.0, pltpu.roll(tt[i], SUBLANES - m, 0))
            et[i] = et[i] * x
            gt[i] = y if gt[i] is None else gt[i] * y
            tt[i] = tt[i] * (x * y)
        level += 1
    mt = 1
    while mt < nt:
        is_second = lambda i: (i // mt) % 2 == 1
        second_tiles = [i for i in range(nt) if is_second(i)]
        a = cat([qt[i] * et[i] for i in second_tiles]).astype(BF16)
        b = cat([kt[i] if is_second(i) else kt[i] * gt[i] for i in range(nt)]).astype(BF16)
        prod = _dot_nt(a, b)
        new_t = list(tt)
        for j in range(0, nt, 2 * mt):
            t_first, t_second = tt[j], tt[j + mt]
            both = t_first * t_second
            for i in range(j, j + mt):
                gt[i] = gt[i] * t_second
                new_t[i] = both
            for i in range(j + mt, j + 2 * mt):
                take(prod, level, i, SUBLANES * j, SUBLANES * (j + mt), second_tiles.index(i))
                et[i] = et[i] * t_first
                new_t[i] = both
        tt = new_t
        mt *= 2
        level += 1
    zero = jnp.zeros((SUBLANES, LANES), F32)
    att = cat([jnp.concatenate([zero if p is None else p for p in pieces], axis=1) for pieces in att])
    return att, cat(et), cat(gt), tt[0][0:1, :]


def _hgrn_prompt_kernel(q_ref, fz_ref, v_ref, gs_ref, lbp_ref, gon_ref, lvl_ref, *rest,
                        chunk, n_chunks, heads, slab):
    o_ref, s_ref, st_scr, o_scr = rest[-4:]
    tb = pl.program_id(2)

    @pl.when(tb == 0)
    def _():
        st_scr[...] = jnp.zeros_like(st_scr)

    gon = gon_ref[...]
    o_scr[...] = jnp.zeros_like(o_scr)

    def finish(c):
        rows = pl.ds(pl.multiple_of(c * chunk, chunk), chunk)
        for h in range(heads):
            cols = slice(h * HEAD_A, (h + 1) * HEAD_A)
            y = _rms(o_scr[h], gon) * gs_ref[0, rows, cols].astype(F32)
            o_ref[0, rows, cols] = y.astype(o_ref.dtype)

    def body(c, carry):
        finish(jnp.maximum(c - 1, 0))
        rows = pl.ds(pl.multiple_of(c * chunk, chunk), chunk)
        for h in range(heads):
            cols = slice(h * HEAD_A, (h + 1) * HEAD_A)
            q = q_ref[0, rows, cols].astype(F32)
            v = v_ref[0, rows, cols]
            f, k = _gates(fz_ref[0, rows, cols], lbp_ref[0:1, cols], lbp_ref[1:2, cols])
            att, e, g, tot = _decay_scores(q, k, f, lvl_ref, chunk)
            st = st_scr[h]
            o_scr[h] = _dot_nt((q * e).astype(BF16), st.astype(BF16)) + _dot(att.astype(BF16), v)
            st_scr[h] = tot * st + _dot_tn(v, (k * g).astype(BF16))
        return carry

    lax.fori_loop(0, n_chunks, body, 0)
    finish(n_chunks - 1)

    @pl.when(tb == pl.num_programs(2) - 1)
    def _():
        s_ref[...] = jnp.zeros_like(s_ref)
        for h in range(heads):
            s_ref[slab, 0, h] = st_scr[h].T


def _level_table(chunk):
    t = lax.broadcasted_iota(jnp.int32, (chunk, chunk), 0)
    s = lax.broadcasted_iota(jnp.int32, (chunk, chunk), 1)
    x = t ^ s
    lvl = jnp.zeros((chunk, chunk), jnp.int32)
    m, level = 1, 1
    while m < chunk:
        lvl = jnp.where((x >= m) & (x < 2 * m), level, lvl)
        m *= 2
        level += 1
    return jnp.where(t >= s, lvl, -1)


def _hgrn_prompt(q, fz, v, gs, lbp, gon, states, layer, n_layers, tblk, chunk, heads):
    b, l, di = q.shape
    n_heads = di // HEAD_A
    wd = heads * HEAD_A
    col = lambda i, h, t: (i, t, h)
    in_specs = [
        pl.BlockSpec((1, tblk, wd), col),
        pl.BlockSpec((1, tblk, wd), col),
        pl.BlockSpec((1, tblk, wd), col),
        pl.BlockSpec((1, tblk, wd), col),
        pl.BlockSpec((2, wd), lambda i, h, t: (0, h)),
        pl.BlockSpec((1, HEAD_A), lambda i, h, t: (0, 0)),
        pl.BlockSpec((chunk, chunk), lambda i, h, t: (0, 0)),
    ]
    args = [q, fz, v, gs, lbp, gon.reshape(1, HEAD_A).astype(F32), _level_table(chunk)]
    aliases = {}
    if states is not None:
        in_specs.append(pl.BlockSpec(memory_space=pl.ANY))
        args.append(states)
        aliases = {len(args) - 1: 1}
    return pl.pallas_call(
        functools.partial(_hgrn_prompt_kernel, chunk=chunk, n_chunks=tblk // chunk, heads=heads,
                          slab=layer if states is None else 0),
        grid=(b, n_heads // heads, l // tblk),
        in_specs=in_specs,
        out_specs=[
            pl.BlockSpec((1, tblk, wd), col),
            (pl.BlockSpec((n_layers, 1, heads, HEAD_A, HEAD_A), lambda i, h, t: (0, i, h, 0, 0))
             if states is None else
             pl.BlockSpec((1, 1, heads, HEAD_A, HEAD_A), lambda i, h, t: (layer, i, h, 0, 0))),
        ],
        out_shape=[
            jax.ShapeDtypeStruct((b, l, di), BF16),
            jax.ShapeDtypeStruct((n_layers, b, n_heads, HEAD_A, HEAD_A), F32),
        ],
        scratch_shapes=[pltpu.VMEM((heads, HEAD_A, HEAD_A), F32), pltpu.VMEM((heads, chunk, HEAD_A), F32)],
        input_output_aliases=aliases,
        compiler_params=pltpu.CompilerParams(
            dimension_semantics=("arbitrary", "arbitrary", "arbitrary"),
            vmem_limit_bytes=VMEM_LIMIT),
        name="hgrn_prompt",
    )(*args)


def _hgrn_sample_kernel(q_ref, fz_ref, v_ref, gs_ref, s0_ref, lbp_ref, gon_ref, *rest, seqs, steps, slab,
                        fill):
    o_ref, s_ref = rest[-2:]
    gon = gon_ref[...]
    row = lax.broadcasted_iota(jnp.int32, (seqs, steps, HEAD_A), 1)
    prow = lax.broadcasted_iota(jnp.int32, (HEAD_A, HEAD_A), 0)
    pad = jnp.zeros((HEAD_A - steps, HEAD_A), F32)
    roll = lambda x, n: pltpu.roll(x, n, 1)

    q = q_ref[...]
    v = v_ref[...]
    f, k = _gates(fz_ref[...], lbp_ref[0:1, :], lbp_ref[1:2, :])
    o = jnp.sum(q * k, axis=-1, keepdims=True) * v
    w = f
    for j in range(1, steps):
        p = jnp.where(row >= j, q * roll(k, j) * w, 0.0)
        o = o + jnp.sum(p, axis=-1, keepdims=True) * roll(v, j)
        w = w * roll(f, j)
    e = f
    g = jnp.where(row < steps - 1, roll(f, steps - 1), 1.0)
    sh = 1
    while sh < steps:
        e = e * jnp.where(row >= sh, roll(e, sh), 1.0)
        g = g * jnp.where(row < steps - sh, roll(g, steps - sh), 1.0)
        sh *= 2
    qe = (q * e).astype(BF16)
    kg = k * g
    if fill:
        s_ref[...] = jnp.zeros_like(s_ref)
    outs = []
    for i in range(seqs):
        s0 = s0_ref[0, i, 0]
        outs.append(o[i] + _dot(qe[i], s0.astype(BF16)))
        z = jnp.where(prow == steps, e[i, steps - 1:steps, :], jnp.concatenate([kg[i], pad], 0))
        zt = z.T
        vp = jnp.concatenate([v[i], pad], 0)
        s_ref[slab, i, 0] = zt[:, steps:steps + 1] * s0 + _dot(zt.astype(BF16), vp.astype(BF16))
    o_ref[...] = (_rms(jnp.stack(outs), gon) * gs_ref[...]).astype(o_ref.dtype)


def _hgrn_sample(q, fz, v, gs, state_in, lbp, gon, states, layer, seqs):
    b, steps, di = q.shape
    heads = di // HEAD_A
    col = lambda i, h: (i, 0, h)
    st = lambda i, h: (layer, i, h, 0, 0)
    in_specs = [
        pl.BlockSpec((seqs, steps, HEAD_A), col),
        pl.BlockSpec((seqs, steps, HEAD_A), col),
        pl.BlockSpec((seqs, steps, HEAD_A), col),
        pl.BlockSpec((seqs, steps, HEAD_A), col),
        pl.BlockSpec((1, seqs, 1, HEAD_A, HEAD_A), st),
        pl.BlockSpec((2, HEAD_A), lambda i, h: (0, h)),
        pl.BlockSpec((1, HEAD_A), lambda i, h: (0, 0)),
    ]
    args = [q, fz, v, gs, state_in, lbp, gon.reshape(1, HEAD_A).astype(F32)]
    aliases = {}
    if states is not None:
        in_specs.append(pl.BlockSpec(memory_space=pl.ANY))
        args.append(states)
        aliases = {len(args) - 1: 1}
    return pl.pallas_call(
        functools.partial(_hgrn_sample_kernel, seqs=seqs, steps=steps, slab=layer if states is None else 0,
                          fill=states is None),
        grid=(b // seqs, heads),
        in_specs=in_specs,
        out_specs=[
            pl.BlockSpec((seqs, steps, HEAD_A), col),
            (pl.BlockSpec((state_in.shape[0], seqs, 1, HEAD_A, HEAD_A), lambda i, h: (0, i, h, 0, 0))
             if states is None else pl.BlockSpec((1, seqs, 1, HEAD_A, HEAD_A), st)),
        ],
        out_shape=[
            jax.ShapeDtypeStruct((b, steps, di), BF16),
            jax.ShapeDtypeStruct(state_in.shape, F32),
        ],
        input_output_aliases=aliases,
        compiler_params=pltpu.CompilerParams(
            dimension_semantics=("arbitrary", "arbitrary"), vmem_limit_bytes=VMEM_LIMIT),
        name="hgrn_sample",
    )(*args)


def _swa_prompt_kernel(sink_ref, q_ref, kvc_ref, kvp_ref, gs_ref, o_ref, *, kv_heads, windows):
    nb = pl.program_id(1)
    kw = kv_heads * HEAD_B
    lane_k = lax.broadcasted_iota(jnp.int32, (2 * WINDOW, LANES), 1)
    lane_q = lax.broadcasted_iota(jnp.int32, (WINDOW, LANES), 1)
    t_i = lax.broadcasted_iota(jnp.int32, (WINDOW, 2 * WINDOW), 0)
    s_i = lax.broadcasted_iota(jnp.int32, (WINDOW, 2 * WINDOW), 1)
    d = t_i + WINDOW - s_i
    band = (d >= 0) & (d <= WINDOW)
    zero = jnp.zeros((), BF16)
    pairs_per_group = GROUP_B // 2
    for wdw in range(windows):
        rows = slice(wdw * WINDOW, (wdw + 1) * WINDOW)
        prev = kvp_ref[0] if wdw == 0 else kvc_ref[0, (wdw - 1) * WINDOW:wdw * WINDOW]
        kv2 = jnp.concatenate([prev, kvc_ref[0, rows]], axis=0).astype(BF16)
        valid = band & ((s_i >= WINDOW) | (nb > 0)) if wdw == 0 else band
        for j in range(kv_heads // 2):
            kk = kv2[:, j * LANES:(j + 1) * LANES]
            vv = kv2[:, kw + j * LANES:kw + (j + 1) * LANES]
            kk_sw = pltpu.roll(kk.astype(F32), HEAD_B, 1).astype(BF16)
            vv_sw = pltpu.roll(vv.astype(F32), HEAD_B, 1).astype(BF16)
            for par in range(2):
                g = 2 * j + par
                if par == 0:
                    k2 = jnp.where(lane_k < HEAD_B, kk, kk_sw)
                    v_lo = jnp.where(lane_k < HEAD_B, vv, zero)
                    v_hi = jnp.where(lane_k >= HEAD_B, vv_sw, zero)
                else:
                    k2 = jnp.where(lane_k >= HEAD_B, kk, kk_sw)
                    v_lo = jnp.where(lane_k < HEAD_B, vv_sw, zero)
                    v_hi = jnp.where(lane_k >= HEAD_B, vv, zero)
                qs, sinks = [], []
                for half in range(2):
                    for pr in range(pairs_per_group):
                        pidx = g * pairs_per_group + pr
                        qp = q_ref[0, rows, pidx * LANES:(pidx + 1) * LANES]
                        qs.append(jnp.where((lane_q >= HEAD_B) == (half == 1), qp, zero))
                        sinks.append(jnp.full((WINDOW, LANES), sink_ref[2 * pidx + half] * LOG2E, F32))
                sink = jnp.stack(sinks)
                s = _dot_nt(jnp.concatenate(qs, axis=0), k2).reshape(GROUP_B, WINDOW, 2 * WINDOW)
                s = jnp.where(valid[None], s, -jnp.inf)
                mx = jnp.broadcast_to(jnp.max(s, axis=-1, keepdims=True), sink.shape)
                p = jnp.exp2(s - jnp.concatenate([mx, mx], axis=-1))
                den = jnp.sum(p, axis=-1, keepdims=True) + jnp.exp2(sink - mx)
                p = p.astype(BF16).reshape(GROUP_B * WINDOW, 2 * WINDOW)
                rden = (1.0 / den).reshape(2, pairs_per_group * WINDOW, LANES)
                half_rows = pairs_per_group * WINDOW
                o = (_dot(p[:half_rows], v_lo) * rden[0] + _dot(p[half_rows:], v_hi) * rden[1])
                for pr in range(pairs_per_group):
                    pidx = g * pairs_per_group + pr
                    cols = slice(pidx * LANES, (pidx + 1) * LANES)
                    gs = gs_ref[0, rows, cols].astype(F32)
                    o_ref[0, rows, cols] = (o[pr * WINDOW:(pr + 1) * WINDOW] * gs).astype(o_ref.dtype)


def _swa_prompt(q, kv, gs, sinks, windows):
    b, l, di = q.shape
    kvw = kv.shape[-1]
    kv_heads = kvw // (2 * HEAD_B)
    tq = windows * WINDOW
    return pl.pallas_call(
        functools.partial(_swa_prompt_kernel, kv_heads=kv_heads, windows=windows),
        grid=(b, l // tq),
        in_specs=[
            pl.BlockSpec(memory_space=pltpu.SMEM),
            pl.BlockSpec((1, tq, di), lambda i, n: (i, n, 0)),
            pl.BlockSpec((1, tq, kvw), lambda i, n: (i, n, 0)),
            pl.BlockSpec((1, WINDOW, kvw), lambda i, n: (i, jnp.maximum(n * windows - 1, 0), 0)),
            pl.BlockSpec((1, tq, di), lambda i, n: (i, n, 0)),
        ],
        out_specs=pl.BlockSpec((1, tq, di), lambda i, n: (i, n, 0)),
        out_shape=jax.ShapeDtypeStruct((b, l, di), BF16),
        compiler_params=pltpu.CompilerParams(
            dimension_semantics=("arbitrary", "arbitrary"), vmem_limit_bytes=VMEM_LIMIT),
        name="swa_prompt",
    )(sinks.astype(F32), q, kv, kv, gs)


def _swa_sample_kernel(sink_ref, q_ref, kvn_ref, cache_ref, gs_ref, o_ref, *, seqs, steps, kv_heads):
    kw = kv_heads * HEAD_B
    pairs = GROUP_B // 2
    rows = GROUP_B * steps
    half_rows = pairs * steps
    lane_q = lax.broadcasted_iota(jnp.int32, (steps, LANES), 1)
    lane_k = lax.broadcasted_iota(jnp.int32, (WINDOW, LANES), 1)
    t_i = lax.broadcasted_iota(jnp.int32, (rows, WINDOW), 0) % steps
    c_i = lax.broadcasted_iota(jnp.int32, (rows, WINDOW), 1)
    valid_c = c_i >= t_i
    valid_n = c_i <= t_i
    pad = jnp.zeros((WINDOW - steps, LANES), F32)
    blocks = [(i, g) for i in range(seqs) for g in range(kv_heads)]

    def kv_tiles(i, g, off):
        col = slice(off + (g // 2) * LANES, off + (g // 2 + 1) * LANES)
        xs = (cache_ref[i, :, col], jnp.concatenate([kvn_ref[i, :, col], pad], 0))
        return [(x, pltpu.roll(x, HEAD_B, 1)) for x in xs]

    scores, sinks = [], []
    for i, g in blocks:
        own_low = g % 2 == 0
        qs = []
        for half in range(2):
            for pr in range(pairs):
                pidx = g * pairs + pr
                tile = q_ref[i, :, pidx * LANES:(pidx + 1) * LANES]
                qs.append(jnp.where((lane_q >= HEAD_B) == (half == 1), tile, 0.0))
                sinks.append(jnp.full((steps, LANES), sink_ref[2 * pidx + half] * LOG2E, F32))
        q = jnp.concatenate(qs, axis=0).astype(BF16)
        kc, kn = [jnp.where((lane_k < HEAD_B) == own_low, x, sw).astype(BF16) for x, sw in kv_tiles(i, g, 0)]
        sc = jnp.where(valid_c, _dot_nt(q, kc), -jnp.inf)
        sn = jnp.where(valid_n, _dot_nt(q, kn), -jnp.inf)
        scores.append(jnp.concatenate([sc, sn], axis=1))
    s = jnp.concatenate(scores, axis=0)
    sink = jnp.concatenate(sinks, axis=0)
    mx = jnp.broadcast_to(jnp.max(s, axis=-1, keepdims=True), sink.shape)
    p = jnp.exp2(s - jnp.concatenate([mx, mx], axis=1))
    rden = 1.0 / (jnp.sum(p, axis=-1, keepdims=True) + jnp.exp2(sink - mx))
    p = p.astype(BF16)
    for n, (i, g) in enumerate(blocks):
        own_low = g % 2 == 0
        o = None
        for half in range(2):
            r = slice(n * rows + half * half_rows, n * rows + (half + 1) * half_rows)
            vc, vn = [jnp.where((lane_k >= HEAD_B) == (half == 1), x if own_low == (half == 0) else sw, 0.0)
                      .astype(BF16) for x, sw in kv_tiles(i, g, kw)]
            oh = (_dot(p[r, :WINDOW], vc) + _dot(p[r, WINDOW:], vn)) * rden[r]
            o = oh if o is None else o + oh
        for pr in range(pairs):
            cols = slice((g * pairs + pr) * LANES, (g * pairs + pr + 1) * LANES)
            o_ref[i, :, cols] = o[pr * steps:(pr + 1) * steps] * gs_ref[i, :, cols]


def _swa_sample(q, kv_new, cache, gs, sinks, seqs):
    b, steps, di = q.shape
    kvw = kv_new.shape[-1]
    kv_heads = kvw // (2 * HEAD_B)
    return pl.pallas_call(
        functools.partial(_swa_sample_kernel, seqs=seqs, steps=steps, kv_heads=kv_heads),
        grid=(b // seqs,),
        in_specs=[
            pl.BlockSpec(memory_space=pltpu.SMEM),
            pl.BlockSpec((seqs, steps, di), lambda i: (i, 0, 0)),
            pl.BlockSpec((seqs, steps, kvw), lambda i: (i, 0, 0)),
            pl.BlockSpec((seqs, WINDOW, kvw), lambda i: (i, 0, 0)),
            pl.BlockSpec((seqs, steps, di), lambda i: (i, 0, 0)),
        ],
        out_specs=pl.BlockSpec((seqs, steps, di), lambda i: (i, 0, 0)),
        out_shape=jax.ShapeDtypeStruct((b, steps, di), F32),
        compiler_params=pltpu.CompilerParams(
            dimension_semantics=("arbitrary",), vmem_limit_bytes=VMEM_LIMIT),
        name="swa_sample",
    )(sinks.astype(F32), q, kv_new, cache, gs)


def _lower_bound_params(p):
    lb = jnp.cumsum(jax.nn.softmax(p.astype(F32), axis=0), axis=0)
    lb = lb - lb[:1]
    return jnp.stack([lb, 1.0 - lb], axis=1)


def kernel(x_prompt, x_sample, state_hgrn, cache_kv_window, w_in_a, w_out_a, norm_a, onorm_a,
           lower_bounds_a, norm_kv, w_kv, w_in_b, w_out_b, norm_b, sinks_b, norm_f):
    bp, lp, dm = x_prompt.shape
    bs, ls, _ = x_sample.shape
    n_a = w_in_a.shape[0]
    n_b = w_in_b.shape[0]
    di = w_out_a.shape[1]
    kvw = w_kv.shape[1]
    kv_heads = kvw // (2 * HEAD_B)
    wb = cache_kv_window.shape[1]
    assert wb == WINDOW and lp % WINDOW == 0 and ls <= WINDOW

    lbp = _lower_bound_params(lower_bounds_a)
    hp = x_prompt.reshape(bp * lp, dm)
    hs = x_sample.reshape(bs * ls, dm)
    cache = cache_kv_window.reshape(bs, wb, kvw)

    state_prompt = state_sample = None
    kv_p = kv_s = None
    w_in_a16 = w_in_a.astype(BF16)
    for layer in range(n_a):
        project = (norm_kv, w_kv) if layer == n_a - 1 else None
        splits = dict(widths=(di,) * 4, epilogues=("silu_a", "none", "none", "silu"))
        parts = _norm_proj(hp, norm_a[layer], w_in_a16, dtypes=(BF16, F32, BF16, BF16), tm=TILES["in_proj_a_rows"],
                           layer=layer, **splits)
        og, state_prompt = _hgrn_prompt(
            *(a.reshape(bp, lp, di) for a in parts), lbp[layer], onorm_a[layer],
            state_prompt, layer, n_a, tblk=min(TILES["recurrence_tokens"], lp),
            chunk=min(TILES["recurrence_chunk"], lp), heads=TILES["recurrence_heads"])
        hp = _out_proj(og.reshape(bp * lp, di), w_out_a, hp, tm=TILES["out_proj_rows"], layer=layer, project=project)

        parts = _norm_proj(hs, norm_a[layer], w_in_a16, dtypes=(F32,) * 4, tm=TILES["sample_rows"], layer=layer, **splits)
        og, state_sample = _hgrn_sample(
            *(a.reshape(bs, ls, di) for a in parts), state_hgrn, lbp[layer], onorm_a[layer],
            state_sample, layer, seqs=min(TILES["sample_recurrence_seqs"], bs))
        hs = _out_proj(og.reshape(bs * ls, di), w_out_a, hs, tm=TILES["sample_rows"], layer=layer, project=project)
    hp, kv_p = hp
    hs, kv_s = hs
    kv_p = kv_p.reshape(bp, lp, kvw)
    kv_s = kv_s.reshape(bs, ls, kvw)

    for j in range(n_b):
        g_final = norm_f if j == n_b - 1 else None
        splits = dict(widths=(di, di), epilogues=("scale_b", "silu"))
        q, gs = _norm_proj(hp, norm_b[j], w_in_b, dtypes=(BF16, BF16), tm=TILES["in_proj_b_rows"], layer=j, **splits)
        o = _swa_prompt(q.reshape(bp, lp, di), kv_p, gs.reshape(bp, lp, di), sinks_b[j],
                        windows=TILES["attention_windows"])
        hp = _out_proj(o.reshape(bp * lp, di), w_out_b, hp, tm=TILES["out_proj_rows"], layer=j, norm=g_final)

        q, gs = _norm_proj(hs, norm_b[j], w_in_b, dtypes=(F32, F32), tm=TILES["sample_rows"], layer=j, **splits)
        o = _swa_sample(q.reshape(bs, ls, di), kv_s, cache, gs.reshape(bs, ls, di), sinks_b[j],
                        seqs=min(TILES["sample_attention_seqs"], bs))
        hs = _out_proj(o.reshape(bs * ls, di), w_out_b, hs, tm=TILES["sample_rows"], norm=g_final, layer=j)

    kv_shape = (2, kv_heads, HEAD_B)
    y_prompt = hp.reshape(bp, lp, dm)
    y_sample = hs.reshape(bs, ls, dm)
    kv_window_prompt = kv_p[:, lp - min(WINDOW, lp):].reshape(bp, min(WINDOW, lp), *kv_shape)
    kv_window_sample = jnp.concatenate([cache, kv_s], axis=1)[:, ls:].reshape(bs, wb, *kv_shape)
    return (y_prompt, y_sample, state_prompt, state_sample, kv_window_prompt, kv_window_sample)
```

```python
import functools

import jax
import jax.numpy as jnp
from jax import lax
from jax.experimental import pallas as pl
from jax.experimental.pallas import tpu as pltpu

F32 = jnp.float32
BF16 = jnp.bfloat16

EPS = 1e-6
HEAD_A = 128
HEAD_B = 64
GROUP_B = 8
WINDOW = 128
LANES = 128
VMEM_LIMIT = 56 * 1024 * 1024
LOG2E = 1.4426950408889634
MXU_COLS = 256

TILES = dict(
    in_proj_a_rows=512,
    in_proj_b_rows=1024,
    out_proj_rows=1024,
    sample_rows=1024,
    recurrence_tokens=2048,
    recurrence_chunk=256,
    recurrence_heads=4,
    attention_windows=4,
    sample_recurrence_seqs=64,
    sample_attention_seqs=32,
)


def _dot(a, b):
    return jnp.dot(a, b, preferred_element_type=F32)


def _dot_nt(a, b):
    return lax.dot_general(a, b, (((1,), (1,)), ((), ())), preferred_element_type=F32)


def _dot_tn(a, b):
    return lax.dot_general(a, b, (((0,), (0,)), ((), ())), preferred_element_type=F32)


def _silu(x):
    return x * (0.5 * jnp.tanh(0.5 * x) + 0.5)


def _rms(x, g):
    return x * lax.rsqrt(jnp.mean(x * x, axis=-1, keepdims=True) + EPS) * g


def _gates(fz, lb, one_m_lb):
    e = jnp.exp(-jnp.abs(fz))
    r = 1.0 / (1.0 + e)
    er = e * r
    pos = fz >= 0.0
    f = lb + one_m_lb * jnp.where(pos, r, er)
    key = one_m_lb * jnp.where(pos, er, r)
    return f, key


_EPILOGUES = {
    "none": lambda y: y,
    "silu": _silu,
    "silu_a": lambda y: _silu(y) * HEAD_A ** -0.5,
    "scale_b": lambda y: y * (HEAD_B ** -0.5 * LOG2E),
}


def _bf16_weight(w_ref, scr):
    if not scr:
        return w_ref

    @pl.when(pl.program_id(0) == 0)
    def _():
        scr[0][...] = w_ref[...].astype(BF16)

    return scr[0]


def _weight_scratch(w):
    return [] if w.dtype == BF16 else [pltpu.VMEM(w.shape[-2:], BF16)]


def _norm_proj_kernel(x_ref, g_ref, w_ref, *rest, widths, epilogues):
    out_refs, scr = rest[:len(widths)], rest[len(widths):]
    w = _bf16_weight(w_ref, scr)
    h = _rms(x_ref[...], g_ref[...]).astype(BF16)
    off = 0
    for o_ref, wd, ep in zip(out_refs, widths, epilogues):
        o_ref[...] = _EPILOGUES[ep](_dot(h, w[:, off:off + wd])).astype(o_ref.dtype)
        off += wd


def _norm_proj(x2d, g, w, widths, epilogues, dtypes, tm, layer=0):
    m, d = x2d.shape
    n = w.shape[-1]
    tm = min(tm, m)
    assert sum(widths) == n and m % tm == 0
    return pl.pallas_call(
        functools.partial(_norm_proj_kernel, widths=tuple(widths), epilogues=tuple(epilogues)),
        grid=(m // tm,),
        in_specs=[
            pl.BlockSpec((tm, d), lambda i: (i, 0)),
            pl.BlockSpec((1, d), lambda i: (0, 0)),
            pl.BlockSpec((None, d, n), lambda i: (layer, 0, 0), pipeline_mode=pl.Buffered(1)),
        ],
        out_specs=[pl.BlockSpec((tm, wd), lambda i: (i, 0)) for wd in widths],
        out_shape=[jax.ShapeDtypeStruct((m, wd), dt) for wd, dt in zip(widths, dtypes)],
        scratch_shapes=_weight_scratch(w),
        compiler_params=pltpu.CompilerParams(
            dimension_semantics=("arbitrary",), vmem_limit_bytes=VMEM_LIMIT),
        name="norm_proj",
    )(x2d, g.reshape(1, d).astype(F32), w)


def _out_proj_kernel(a_ref, w_ref, x_ref, g_ref, w2_ref, *rest, mode):
    n_out = 2 if mode == "project" else 1
    out_refs, scr = rest[:n_out], rest[n_out:]
    w = _bf16_weight(w_ref, scr)
    a = a_ref[...].astype(BF16)
    y_ref = out_refs[0]
    for c0 in range(0, y_ref.shape[1], MXU_COLS):
        c = slice(c0, c0 + MXU_COLS)
        y_ref[:, c] = x_ref[:, c] + _dot(a, w[:, c])
    if mode == "norm":
        y_ref[...] = _rms(y_ref[...], g_ref[...])
    if mode == "project":
        out_refs[1][...] = _dot(_rms(y_ref[...], g_ref[...]).astype(BF16), w2_ref[...].astype(BF16))


def _out_proj(a2d, w, x2d, tm, layer, norm=None, project=None):
    m, k = a2d.shape
    d = w.shape[-1]
    tm = min(tm, m)
    mode = "norm" if norm is not None else "project" if project is not None else "plain"
    g, w2 = (norm, None) if mode == "norm" else project if mode == "project" else (jnp.ones((d,), F32), None)
    if w2 is None:
        w2 = jnp.zeros((8, LANES), F32)
    out_specs = [pl.BlockSpec((tm, d), lambda i: (i, 0))]
    out_shape = [jax.ShapeDtypeStruct((m, d), F32)]
    if mode == "project":
        out_specs.append(pl.BlockSpec((tm, w2.shape[1]), lambda i: (i, 0)))
        out_shape.append(jax.ShapeDtypeStruct((m, w2.shape[1]), F32))
    outs = pl.pallas_call(
        functools.partial(_out_proj_kernel, mode=mode),
        grid=(m // tm,),
        in_specs=[
            pl.BlockSpec((tm, k), lambda i: (i, 0)),
            pl.BlockSpec((None, k, d), lambda i: (layer, 0, 0), pipeline_mode=pl.Buffered(1)),
            pl.BlockSpec((tm, d), lambda i: (i, 0)),
            pl.BlockSpec((1, d), lambda i: (0, 0)),
            pl.BlockSpec(w2.shape, lambda i: (0, 0), pipeline_mode=pl.Buffered(1)),
        ],
        out_specs=out_specs,
        out_shape=out_shape,
        scratch_shapes=_weight_scratch(w),
        compiler_params=pltpu.CompilerParams(
            dimension_semantics=("arbitrary",), vmem_limit_bytes=VMEM_LIMIT),
        name="out_proj",
    )(a2d, w, x2d, g.reshape(1, d).astype(F32), w2)
    return outs if mode == "project" else outs[0]


SUBLANES = 8


def _decay_scores(q, k, f, lvl_ref, chunk):
    nt = chunk // SUBLANES
    tiles = lambda x: [x[SUBLANES * i:SUBLANES * (i + 1)] for i in range(nt)]
    cat = lambda xs: jnp.concatenate(xs, axis=0)
    row = lax.broadcasted_iota(jnp.int32, (SUBLANES, HEAD_A), 0)
    qt, kt = tiles(q), tiles(k)
    et, tt = tiles(f), tiles(f)
    gt = [None] * nt
    att = [[None] * (chunk // LANES) for _ in range(nt)]

    def take(prod, level, i, lo, hi, p=None):
        r = slice(SUBLANES * i, SUBLANES * (i + 1))
        rp = r if p is None else slice(SUBLANES * p, SUBLANES * (p + 1))
        for j in range(lo // LANES, (hi - 1) // LANES + 1):
            c = slice(LANES * j, LANES * (j + 1))
            keep = 0.0 if att[i][j] is None else att[i][j]
            att[i][j] = jnp.where(lvl_ref[r, c] == level, prod[rp, c], keep)

    k16 = k.astype(BF16)
    prod = _dot_nt(q.astype(BF16), k16)
    for i in range(nt):
        take(prod, 0, i, SUBLANES * i, SUBLANES * (i + 1))
    level = 1
    for m in (1, 2, 4):
        a = cat([qt[i] * et[i] for i in range(nt)]).astype(BF16)
        b = k16 if m == 1 else cat([kt[i] * gt[i] for i in range(nt)]).astype(BF16)
        prod = _dot_nt(a, b)
        second = (row & m) != 0
        for i in range(nt):
            take(prod, level, i, SUBLANES * i, SUBLANES * (i + 1))
            x = jnp.where(second, pltpu.roll(tt[i], m, 0), 1.0)
            y = jnp.where(second, 1.0, pltpu.roll(tt[i], SUBLANES - m, 0))
            et[i] = et[i] * x
            gt[i] = y if gt[i] is None else gt[i] * y
            tt[i] = tt[i] * (x * y)
        level += 1
    mt = 1
    while mt < nt:
        is_second = lambda i: (i // mt) % 2 == 1
        second_tiles = [i for i in range(nt) if is_second(i)]
        a = cat([qt[i] * et[i] for i in second_tiles]).astype(BF16)
        b = cat([kt[i] if is_second(i) else kt[i] * gt[i] for i in range(nt)]).astype(BF16)
        prod = _dot_nt(a, b)
        new_t = list(tt)
        for j in range(0, nt, 2 * mt):
            t_first, t_second = tt[j], tt[j + mt]
            both = t_first * t_second
            for i in range(j, j + mt):
                gt[i] = gt[i] * t_second
                new_t[i] = both
            for i in range(j + mt, j + 2 * mt):
                take(prod, level, i, SUBLANES * j, SUBLANES * (j + mt), second_tiles.index(i))
                et[i] = et[i] * t_first
                new_t[i] = both
        tt = new_t
        mt *= 2
        level += 1
    zero = jnp.zeros((SUBLANES, LANES), F32)
    att = cat([jnp.concatenate([zero if p is None else p for p in pieces], axis=1) for pieces in att])
    return att, cat(et), cat(gt), tt[0][0:1, :]


def _hgrn_prompt_kernel(q_ref, fz_ref, v_ref, gs_ref, lbp_ref, gon_ref, lvl_ref, *rest,
                        chunk, n_chunks, heads, slab):
    o_ref, s_ref, st_scr, o_scr = rest[-4:]
    tb = pl.program_id(2)

    @pl.when(tb == 0)
    def _():
        st_scr[...] = jnp.zeros_like(st_scr)

    gon = gon_ref[...]
    o_scr[...] = jnp.zeros_like(o_scr)

    def finish(c):
        rows = pl.ds(pl.multiple_of(c * chunk, chunk), chunk)
        for h in range(heads):
            cols = slice(h * HEAD_A, (h + 1) * HEAD_A)
            y = _rms(o_scr[h], gon) * gs_ref[0, rows, cols].astype(F32)
            o_ref[0, rows, cols] = y.astype(o_ref.dtype)

    def body(c, carry):
        finish(jnp.maximum(c - 1, 0))
        rows = pl.ds(pl.multiple_of(c * chunk, chunk), chunk)
        for h in range(heads):
            cols = slice(h * HEAD_A, (h + 1) * HEAD_A)
            q = q_ref[0, rows, cols].astype(F32)
            v = v_ref[0, rows, cols]
            f, k = _gates(fz_ref[0, rows, cols], lbp_ref[0:1, cols], lbp_ref[1:2, cols])
            att, e, g, tot = _decay_scores(q, k, f, lvl_ref, chunk)
            st = st_scr[h]
            o_scr[h] = _dot_nt((q * e).astype(BF16), st.astype(BF16)) + _dot(att.astype(BF16), v)
            st_scr[h] = tot * st + _dot_tn(v, (k * g).astype(BF16))
        return carry

    lax.fori_loop(0, n_chunks, body, 0)
    finish(n_chunks - 1)

    @pl.when(tb == pl.num_programs(2) - 1)
    def _():
        s_ref[...] = jnp.zeros_like(s_ref)
        for h in range(heads):
            s_ref[slab, 0, h] = st_scr[h].T


def _level_table(chunk):
    t = lax.broadcasted_iota(jnp.int32, (chunk, chunk), 0)
    s = lax.broadcasted_iota(jnp.int32, (chunk, chunk), 1)
    x = t ^ s
    lvl = jnp.zeros((chunk, chunk), jnp.int32)
    m, level = 1, 1
    while m < chunk:
        lvl = jnp.where((x >= m) & (x < 2 * m), level, lvl)
        m *= 2
        level += 1
    return jnp.where(t >= s, lvl, -1)


def _hgrn_prompt(q, fz, v, gs, lbp, gon, states, layer, n_layers, tblk, chunk, heads):
    b, l, di = q.shape
    n_heads = di // HEAD_A
    wd = heads * HEAD_A
    col = lambda i, h, t: (i, t, h)
    in_specs = [
        pl.BlockSpec((1, tblk, wd), col),
        pl.BlockSpec((1, tblk, wd), col),
        pl.BlockSpec((1, tblk, wd), col),
        pl.BlockSpec((1, tblk, wd), col),
        pl.BlockSpec((2, wd), lambda i, h, t: (0, h)),
        pl.BlockSpec((1, HEAD_A), lambda i, h, t: (0, 0)),
        pl.BlockSpec((chunk, chunk), lambda i, h, t: (0, 0)),
    ]
    args = [q, fz, v, gs, lbp, gon.reshape(1, HEAD_A).astype(F32), _level_table(chunk)]
    aliases = {}
    if states is not None:
        in_specs.append(pl.BlockSpec(memory_space=pl.ANY))
        args.append(states)
        aliases = {len(args) - 1: 1}
    return pl.pallas_call(
        functools.partial(_hgrn_prompt_kernel, chunk=chunk, n_chunks=tblk // chunk, heads=heads,
                          slab=layer if states is None else 0),
        grid=(b, n_heads // heads, l // tblk),
        in_specs=in_specs,
        out_specs=[
            pl.BlockSpec((1, tblk, wd), col),
            (pl.BlockSpec((n_layers, 1, heads, HEAD_A, HEAD_A), lambda i, h, t: (0, i, h, 0, 0))
             if states is None else
             pl.BlockSpec((1, 1, heads, HEAD_A, HEAD_A), lambda i, h, t: (layer, i, h, 0, 0))),
        ],
        out_shape=[
            jax.ShapeDtypeStruct((b, l, di), BF16),
            jax.ShapeDtypeStruct((n_layers, b, n_heads, HEAD_A, HEAD_A), F32),
        ],
        scratch_shapes=[pltpu.VMEM((heads, HEAD_A, HEAD_A), F32), pltpu.VMEM((heads, chunk, HEAD_A), F32)],
        input_output_aliases=aliases,
        compiler_params=pltpu.CompilerParams(
            dimension_semantics=("arbitrary", "arbitrary", "arbitrary"),
            vmem_limit_bytes=VMEM_LIMIT),
        name="hgrn_prompt",
    )(*args)


def _hgrn_sample_kernel(q_ref, fz_ref, v_ref, gs_ref, s0_ref, lbp_ref, gon_ref, *rest, seqs, steps, slab,
                        fill):
    o_ref, s_ref = rest[-2:]
    gon = gon_ref[...]
    row = lax.broadcasted_iota(jnp.int32, (seqs, steps, HEAD_A), 1)
    prow = lax.broadcasted_iota(jnp.int32, (HEAD_A, HEAD_A), 0)
    pad = jnp.zeros((HEAD_A - steps, HEAD_A), F32)
    roll = lambda x, n: pltpu.roll(x, n, 1)

    q = q_ref[...]
    v = v_ref[...]
    f, k = _gates(fz_ref[...], lbp_ref[0:1, :], lbp_ref[1:2, :])
    o = jnp.sum(q * k, axis=-1, keepdims=True) * v
    w = f
    for j in range(1, steps):
        p = jnp.where(row >= j, q * roll(k, j) * w, 0.0)
        o = o + jnp.sum(p, axis=-1, keepdims=True) * roll(v, j)
        w = w * roll(f, j)
    e = f
    g = jnp.where(row < steps - 1, roll(f, steps - 1), 1.0)
    sh = 1
    while sh < steps:
        e = e * jnp.where(row >= sh, roll(e, sh), 1.0)
        g = g * jnp.where(row < steps - sh, roll(g, steps - sh), 1.0)
        sh *= 2
    qe = (q * e).astype(BF16)
    kg = k * g
    if fill:
        s_ref[...] = jnp.zeros_like(s_ref)
    outs = []
    for i in range(seqs):
        s0 = s0_ref[0, i, 0]
        outs.append(o[i] + _dot(qe[i], s0.astype(BF16)))
        z = jnp.where(prow == steps, e[i, steps - 1:steps, :], jnp.concatenate([kg[i], pad], 0))
        zt = z.T
        vp = jnp.concatenate([v[i], pad], 0)
        s_ref[slab, i, 0] = zt[:, steps:steps + 1] * s0 + _dot(zt.astype(BF16), vp.astype(BF16))
    o_ref[...] = (_rms(jnp.stack(outs), gon) * gs_ref[...]).astype(o_ref.dtype)


def _hgrn_sample(q, fz, v, gs, state_in, lbp, gon, states, layer, seqs):
    b, steps, di = q.shape
    heads = di // HEAD_A
    col = lambda i, h: (i, 0, h)
    st = lambda i, h: (layer, i, h, 0, 0)
    in_specs = [
        pl.BlockSpec((seqs, steps, HEAD_A), col),
        pl.BlockSpec((seqs, steps, HEAD_A), col),
        pl.BlockSpec((seqs, steps, HEAD_A), col),
        pl.BlockSpec((seqs, steps, HEAD_A), col),
        pl.BlockSpec((1, seqs, 1, HEAD_A, HEAD_A), st),
        pl.BlockSpec((2, HEAD_A), lambda i, h: (0, h)),
        pl.BlockSpec((1, HEAD_A), lambda i, h: (0, 0)),
    ]
    args = [q, fz, v, gs, state_in, lbp, gon.reshape(1, HEAD_A).astype(F32)]
    aliases = {}
    if states is not None:
        in_specs.append(pl.BlockSpec(memory_space=pl.ANY))
        args.append(states)
        aliases = {len(args) - 1: 1}
    return pl.pallas_call(
        functools.partial(_hgrn_sample_kernel, seqs=seqs, steps=steps, slab=layer if states is None else 0,
                          fill=states is None),
        grid=(b // seqs, heads),
        in_specs=in_specs,
        out_specs=[
            pl.BlockSpec((seqs, steps, HEAD_A), col),
            (pl.BlockSpec((state_in.shape[0], seqs, 1, HEAD_A, HEAD_A), lambda i, h: (0, i, h, 0, 0))
             if states is None else pl.BlockSpec((1, seqs, 1, HEAD_A, HEAD_A), st)),
        ],
        out_shape=[
            jax.ShapeDtypeStruct((b, steps, di), BF16),
            jax.ShapeDtypeStruct(state_in.shape, F32),
        ],
        input_output_aliases=aliases,
        compiler_params=pltpu.CompilerParams(
            dimension_semantics=("arbitrary", "arbitrary"), vmem_limit_bytes=VMEM_LIMIT),
        name="hgrn_sample",
    )(*args)


def _swa_prompt_kernel(sink_ref, q_ref, kvc_ref, kvp_ref, gs_ref, o_ref, *, kv_heads, windows):
    nb = pl.program_id(1)
    kw = kv_heads * HEAD_B
    lane_k = lax.broadcasted_iota(jnp.int32, (2 * WINDOW, LANES), 1)
    lane_q = lax.broadcasted_iota(jnp.int32, (WINDOW, LANES), 1)
    t_i = lax.broadcasted_iota(jnp.int32, (WINDOW, 2 * WINDOW), 0)
    s_i = lax.broadcasted_iota(jnp.int32, (WINDOW, 2 * WINDOW), 1)
    d = t_i + WINDOW - s_i
    band = (d >= 0) & (d <= WINDOW)
    zero = jnp.zeros((), BF16)
    pairs_per_group = GROUP_B // 2
    for wdw in range(windows):
        rows = slice(wdw * WINDOW, (wdw + 1) * WINDOW)
        prev = kvp_ref[0] if wdw == 0 else kvc_ref[0, (wdw - 1) * WINDOW:wdw * WINDOW]
        kv2 = jnp.concatenate([prev, kvc_ref[0, rows]], axis=0).astype(BF16)
        valid = band & ((s_i >= WINDOW) | (nb > 0)) if wdw == 0 else band
        for j in range(kv_heads // 2):
            kk = kv2[:, j * LANES:(j + 1) * LANES]
            vv = kv2[:, kw + j * LANES:kw + (j + 1) * LANES]
            kk_sw = pltpu.roll(kk.astype(F32), HEAD_B, 1).astype(BF16)
            vv_sw = pltpu.roll(vv.astype(F32), HEAD_B, 1).astype(BF16)
            for par in range(2):
                g = 2 * j + par
                if par == 0:
                    k2 = jnp.where(lane_k < HEAD_B, kk, kk_sw)
                    v_lo = jnp.where(lane_k < HEAD_B, vv, zero)
                    v_hi = jnp.where(lane_k >= HEAD_B, vv_sw, zero)
                else:
                    k2 = jnp.where(lane_k >= HEAD_B, kk, kk_sw)
                    v_lo = jnp.where(lane_k < HEAD_B, vv_sw, zero)
                    v_hi = jnp.where(lane_k >= HEAD_B, vv, zero)
                qs, sinks = [], []
                for half in range(2):
                    for pr in range(pairs_per_group):
                        pidx = g * pairs_per_group + pr
                        qp = q_ref[0, rows, pidx * LANES:(pidx + 1) * LANES]
                        qs.append(jnp.where((lane_q >= HEAD_B) == (half == 1), qp, zero))
                        sinks.append(jnp.full((WINDOW, LANES), sink_ref[2 * pidx + half] * LOG2E, F32))
                sink = jnp.stack(sinks)
                s = _dot_nt(jnp.concatenate(qs, axis=0), k2).reshape(GROUP_B, WINDOW, 2 * WINDOW)
                s = jnp.where(valid[None], s, -jnp.inf)
                mx = jnp.broadcast_to(jnp.max(s, axis=-1, keepdims=True), sink.shape)
                p = jnp.exp2(s - jnp.concatenate([mx, mx], axis=-1))
                den = jnp.sum(p, axis=-1, keepdims=True) + jnp.exp2(sink - mx)
                p = p.astype(BF16).reshape(GROUP_B * WINDOW, 2 * WINDOW)
                rden = (1.0 / den).reshape(2, pairs_per_group * WINDOW, LANES)
                half_rows = pairs_per_group * WINDOW
                o = (_dot(p[:half_rows], v_lo) * rden[0] + _dot(p[half_rows:], v_hi) * rden[1])
                for pr in range(pairs_per_group):
                    pidx = g * pairs_per_group + pr
                    cols = slice(pidx * LANES, (pidx + 1) * LANES)
                    gs = gs_ref[0, rows, cols].astype(F32)
                    o_ref[0, rows, cols] = (o[pr * WINDOW:(pr + 1) * WINDOW] * gs).astype(o_ref.dtype)


def _swa_prompt(q, kv, gs, sinks, windows):
    b, l, di = q.shape
    kvw = kv.shape[-1]
    kv_heads = kvw // (2 * HEAD_B)
    tq = windows * WINDOW
    return pl.pallas_call(
        functools.partial(_swa_prompt_kernel, kv_heads=kv_heads, windows=windows),
        grid=(b, l // tq),
        in_specs=[
            pl.BlockSpec(memory_space=pltpu.SMEM),
            pl.BlockSpec((1, tq, di), lambda i, n: (i, n, 0)),
            pl.BlockSpec((1, tq, kvw), lambda i, n: (i, n, 0)),
            pl.BlockSpec((1, WINDOW, kvw), lambda i, n: (i, jnp.maximum(n * windows - 1, 0), 0)),
            pl.BlockSpec((1, tq, di), lambda i, n: (i, n, 0)),
        ],
        out_specs=pl.BlockSpec((1, tq, di), lambda i, n: (i, n, 0)),
        out_shape=jax.ShapeDtypeStruct((b, l, di), BF16),
        compiler_params=pltpu.CompilerParams(
            dimension_semantics=("arbitrary", "arbitrary"), vmem_limit_bytes=VMEM_LIMIT),
        name="swa_prompt",
    )(sinks.astype(F32), q, kv, kv, gs)


def _swa_sample_kernel(sink_ref, q_ref, kvn_ref, cache_ref, gs_ref, o_ref, *, seqs, steps, kv_heads):
    kw = kv_heads * HEAD_B
    pairs = GROUP_B // 2
    rows = GROUP_B * steps
    half_rows = pairs * steps
    lane_q = lax.broadcasted_iota(jnp.int32, (steps, LANES), 1)
    lane_k = lax.broadcasted_iota(jnp.int32, (WINDOW, LANES), 1)
    t_i = lax.broadcasted_iota(jnp.int32, (rows, WINDOW), 0) % steps
    c_i = lax.broadcasted_iota(jnp.int32, (rows, WINDOW), 1)
    valid_c = c_i >= t_i
    valid_n = c_i <= t_i
    pad = jnp.zeros((WINDOW - steps, LANES), F32)
    blocks = [(i, g) for i in range(seqs) for g in range(kv_heads)]

    def kv_tiles(i, g, off):
        col = slice(off + (g // 2) * LANES, off + (g // 2 + 1) * LANES)
        xs = (cache_ref[i, :, col], jnp.concatenate([kvn_ref[i, :, col], pad], 0))
        return [(x, pltpu.roll(x, HEAD_B, 1)) for x in xs]

    scores, sinks = [], []
    for i, g in blocks:
        own_low = g % 2 == 0
        qs = []
        for half in range(2):
            for pr in range(pairs):
                pidx = g * pairs + pr
                tile = q_ref[i, :, pidx * LANES:(pidx + 1) * LANES]
                qs.append(jnp.where((lane_q >= HEAD_B) == (half == 1), tile, 0.0))
                sinks.append(jnp.full((steps, LANES), sink_ref[2 * pidx + half] * LOG2E, F32))
        q = jnp.concatenate(qs, axis=0).astype(BF16)
        kc, kn = [jnp.where((lane_k < HEAD_B) == own_low, x, sw).astype(BF16) for x, sw in kv_tiles(i, g, 0)]
        sc = jnp.where(valid_c, _dot_nt(q, kc), -jnp.inf)
        sn = jnp.where(valid_n, _dot_nt(q, kn), -jnp.inf)
        scores.append(jnp.concatenate([sc, sn], axis=1))
    s = jnp.concatenate(scores, axis=0)
    sink = jnp.concatenate(sinks, axis=0)
    mx = jnp.broadcast_to(jnp.max(s, axis=-1, keepdims=True), sink.shape)
    p = jnp.exp2(s - jnp.concatenate([mx, mx], axis=1))
    rden = 1.0 / (jnp.sum(p, axis=-1, keepdims=True) + jnp.exp2(sink - mx))
    p = p.astype(BF16)
    for n, (i, g) in enumerate(blocks):
        own_low = g % 2 == 0
        o = None
        for half in range(2):
            r = slice(n * rows + half * half_rows, n * rows + (half + 1) * half_rows)
            vc, vn = [jnp.where((lane_k >= HEAD_B) == (half == 1), x if own_low == (half == 0) else sw, 0.0)
                      .astype(BF16) for x, sw in kv_tiles(i, g, kw)]
            oh = (_dot(p[r, :WINDOW], vc) + _dot(p[r, WINDOW:], vn)) * rden[r]
            o = oh if o is None else o + oh
        for pr in range(pairs):
            cols = slice((g * pairs + pr) * LANES, (g * pairs + pr + 1) * LANES)
            o_ref[i, :, cols] = o[pr * steps:(pr + 1) * steps] * gs_ref[i, :, cols]


def _swa_sample(q, kv_new, cache, gs, sinks, seqs):
    b, steps, di = q.shape
    kvw = kv_new.shape[-1]
    kv_heads = kvw // (2 * HEAD_B)
    return pl.pallas_call(
        functools.partial(_swa_sample_kernel, seqs=seqs, steps=steps, kv_heads=kv_heads),
        grid=(b // seqs,),
        in_specs=[
            pl.BlockSpec(memory_space=pltpu.SMEM),
            pl.BlockSpec((seqs, steps, di), lambda i: (i, 0, 0)),
            pl.BlockSpec((seqs, steps, kvw), lambda i: (i, 0, 0)),
            pl.BlockSpec((seqs, WINDOW, kvw), lambda i: (i, 0, 0)),
            pl.BlockSpec((seqs, steps, di), lambda i: (i, 0, 0)),
        ],
        out_specs=pl.BlockSpec((seqs, steps, di), lambda i: (i, 0, 0)),
        out_shape=jax.ShapeDtypeStruct((b, steps, di), F32),
        compiler_params=pltpu.CompilerParams(
            dimension_semantics=("arbitrary",), vmem_limit_bytes=VMEM_LIMIT),
        name="swa_sample",
    )(sinks.astype(F32), q, kv_new, cache, gs)


def _lower_bound_params(p):
    lb = jnp.cumsum(jax.nn.softmax(p.astype(F32), axis=0), axis=0)
    lb = lb - lb[:1]
    return jnp.stack([lb, 1.0 - lb], axis=1)


def kernel(x_prompt, x_sample, state_hgrn, cache_kv_window, w_in_a, w_out_a, norm_a, onorm_a,
           lower_bounds_a, norm_kv, w_kv, w_in_b, w_out_b, norm_b, sinks_b, norm_f):
    bp, lp, dm = x_prompt.shape
    bs, ls, _ = x_sample.shape
    n_a = w_in_a.shape[0]
    n_b = w_in_b.shape[0]
    di = w_out_a.shape[1]
    kvw = w_kv.shape[1]
    kv_heads = kvw // (2 * HEAD_B)
    wb = cache_kv_window.shape[1]
    assert wb == WINDOW and lp % WINDOW == 0 and ls <= WINDOW

    lbp = _lower_bound_params(lower_bounds_a)
    hp = x_prompt.reshape(bp * lp, dm)
    hs = x_sample.reshape(bs * ls, dm)
    cache = cache_kv_window.reshape(bs, wb, kvw)

    state_prompt = state_sample = None
    kv_p = kv_s = None
    w_in_a16 = w_in_a.astype(BF16)
    for layer in range(n_a):
        project = (norm_kv, w_kv) if layer == n_a - 1 else None
        splits = dict(widths=(di,) * 4, epilogues=("silu_a", "none", "none", "silu"))
        parts = _norm_proj(hp, norm_a[layer], w_in_a16, dtypes=(BF16, F32, BF16, BF16), tm=TILES["in_proj_a_rows"],
                           layer=layer, **splits)
        og, state_prompt = _hgrn_prompt(
            *(a.reshape(bp, lp, di) for a in parts), lbp[layer], onorm_a[layer],
            state_prompt, layer, n_a, tblk=min(TILES["recurrence_tokens"], lp),
            chunk=min(TILES["recurrence_chunk"], lp), heads=TILES["recurrence_heads"])
        hp = _out_proj(og.reshape(bp * lp, di), w_out_a, hp, tm=TILES["out_proj_rows"], layer=layer, project=project)

        parts = _norm_proj(hs, norm_a[layer], w_in_a16, dtypes=(F32,) * 4, tm=TILES["sample_rows"], layer=layer, **splits)
        og, state_sample = _hgrn_sample(
            *(a.reshape(bs, ls, di) for a in parts), state_hgrn, lbp[layer], onorm_a[layer],
            state_sample, layer, seqs=min(TILES["sample_recurrence_seqs"], bs))
        hs = _out_proj(og.reshape(bs * ls, di), w_out_a, hs, tm=TILES["sample_rows"], layer=layer, project=project)
    hp, kv_p = hp
    hs, kv_s = hs
    kv_p = kv_p.reshape(bp, lp, kvw)
    kv_s = kv_s.reshape(bs, ls, kvw)

    for j in range(n_b):
        g_final = norm_f if j == n_b - 1 else None
        splits = dict(widths=(di, di), epilogues=("scale_b", "silu"))
        q, gs = _norm_proj(hp, norm_b[j], w_in_b, dtypes=(BF16, BF16), tm=TILES["in_proj_b_rows"], layer=j, **splits)
        o = _swa_prompt(q.reshape(bp, lp, di), kv_p, gs.reshape(bp, lp, di), sinks_b[j],
                        windows=TILES["attention_windows"])
        hp = _out_proj(o.reshape(bp * lp, di), w_out_b, hp, tm=TILES["out_proj_rows"], layer=j, norm=g_final)

        q, gs = _norm_proj(hs, norm_b[j], w_in_b, dtypes=(F32, F32), tm=TILES["sample_rows"], layer=j, **splits)
        o = _swa_sample(q.reshape(bs, ls, di), kv_s, cache, gs.reshape(bs, ls, di), sinks_b[j],
                        seqs=min(TILES["sample_attention_seqs"], bs))
        hs = _out_proj(o.reshape(bs * ls, di), w_out_b, hs, tm=TILES["sample_rows"], norm=g_final, layer=j)

    kv_shape = (2, kv_heads, HEAD_B)
    y_prompt = hp.reshape(bp, lp, dm)
    y_sample = hs.reshape(bs, ls, dm)
    kv_window_prompt = kv_p[:, lp - min(WINDOW, lp):].reshape(bp, min(WINDOW, lp), *kv_shape)
    kv_window_sample = jnp.concatenate([cache, kv_s], axis=1)[:, ls:].reshape(bs, wb, *kv_shape)
    return (y_prompt, y_sample, state_prompt, state_sample, kv_window_prompt, kv_window_sample)
```

```python
import functools

import jax
import jax.numpy as jnp
from jax import lax
from jax.experimental import pallas as pl
from jax.experimental.pallas import tpu as pltpu

F32 = jnp.float32
BF16 = jnp.bfloat16

EPS = 1e-6
HEAD_A = 128
HEAD_B = 64
GROUP_B = 8
WINDOW = 128
LANES = 128
VMEM_LIMIT = 56 * 1024 * 1024
LOG2E = 1.4426950408889634
MXU_COLS = 256

TILES = dict(
    in_proj_a_rows=512,
    in_proj_b_rows=1024,
    out_proj_rows=1024,
    sample_rows=1024,
    recurrence_tokens=2048,
    recurrence_chunk=256,
    recurrence_heads=4,
    attention_windows=4,
    sample_recurrence_seqs=64,
    sample_attention_seqs=32,
)


def _dot(a, b):
    return jnp.dot(a, b, preferred_element_type=F32)


def _dot_nt(a, b):
    return lax.dot_general(a, b, (((1,), (1,)), ((), ())), preferred_element_type=F32)


def _dot_tn(a, b):
    return lax.dot_general(a, b, (((0,), (0,)), ((), ())), preferred_element_type=F32)


def _silu(x):
    return x * (0.5 * jnp.tanh(0.5 * x) + 0.5)


def _rms(x, g):
    return x * lax.rsqrt(jnp.mean(x * x, axis=-1, keepdims=True) + EPS) * g


def _gates(fz, lb, one_m_lb):
    e = jnp.exp(-jnp.abs(fz))
    r = 1.0 / (1.0 + e)
    er = e * r
    pos = fz >= 0.0
    f = lb + one_m_lb * jnp.where(pos, r, er)
    key = one_m_lb * jnp.where(pos, er, r)
    return f, key


_EPILOGUES = {
    "none": lambda y: y,
    "silu": _silu,
    "silu_a": lambda y: _silu(y) * HEAD_A ** -0.5,
    "scale_b": lambda y: y * (HEAD_B ** -0.5 * LOG2E),
}


def _bf16_weight(w_ref, scr):
    if not scr:
        return w_ref

    @pl.when(pl.program_id(0) == 0)
    def _():
        scr[0][...] = w_ref[...].astype(BF16)

    return scr[0]


def _weight_scratch(w):
    return [] if w.dtype == BF16 else [pltpu.VMEM(w.shape[-2:], BF16)]


def _norm_proj_kernel(x_ref, g_ref, w_ref, *rest, widths, epilogues):
    out_refs, scr = rest[:len(widths)], rest[len(widths):]
    w = _bf16_weight(w_ref, scr)
    h = _rms(x_ref[...], g_ref[...]).astype(BF16)
    off = 0
    for o_ref, wd, ep in zip(out_refs, widths, epilogues):
        o_ref[...] = _EPILOGUES[ep](_dot(h, w[:, off:off + wd])).astype(o_ref.dtype)
        off += wd


def _norm_proj(x2d, g, w, widths, epilogues, dtypes, tm, layer=0):
    m, d = x2d.shape
    n = w.shape[-1]
    tm = min(tm, m)
    assert sum(widths) == n and m % tm == 0
    return pl.pallas_call(
        functools.partial(_norm_proj_kernel, widths=tuple(widths), epilogues=tuple(epilogues)),
        grid=(m // tm,),
        in_specs=[
            pl.BlockSpec((tm, d), lambda i: (i, 0)),
            pl.BlockSpec((1, d), lambda i: (0, 0)),
            pl.BlockSpec((None, d, n), lambda i: (layer, 0, 0), pipeline_mode=pl.Buffered(1)),
        ],
        out_specs=[pl.BlockSpec((tm, wd), lambda i: (i, 0)) for wd in widths],
        out_shape=[jax.ShapeDtypeStruct((m, wd), dt) for wd, dt in zip(widths, dtypes)],
        scratch_shapes=_weight_scratch(w),
        compiler_params=pltpu.CompilerParams(
            dimension_semantics=("arbitrary",), vmem_limit_bytes=VMEM_LIMIT),
        name="norm_proj",
    )(x2d, g.reshape(1, d).astype(F32), w)


def _out_proj_kernel(a_ref, w_ref, x_ref, g_ref, w2_ref, *rest, mode):
    n_out = 2 if mode == "project" else 1
    out_refs, scr = rest[:n_out], rest[n_out:]
    w = _bf16_weight(w_ref, scr)
    a = a_ref[...].astype(BF16)
    y_ref = out_refs[0]
    for c0 in range(0, y_ref.shape[1], MXU_COLS):
        c = slice(c0, c0 + MXU_COLS)
        y_ref[:, c] = x_ref[:, c] + _dot(a, w[:, c])
    if mode == "norm":
        y_ref[...] = _rms(y_ref[...], g_ref[...])
    if mode == "project":
        out_refs[1][...] = _dot(_rms(y_ref[...], g_ref[...]).astype(BF16), w2_ref[...].astype(BF16))


def _out_proj(a2d, w, x2d, tm, layer, norm=None, project=None):
    m, k = a2d.shape
    d = w.shape[-1]
    tm = min(tm, m)
    mode = "norm" if norm is not None else "project" if project is not None else "plain"
    g, w2 = (norm, None) if mode == "norm" else project if mode == "project" else (jnp.ones((d,), F32), None)
    if w2 is None:
        w2 = jnp.zeros((8, LANES), F32)
    out_specs = [pl.BlockSpec((tm, d), lambda i: (i, 0))]
    out_shape = [jax.ShapeDtypeStruct((m, d), F32)]
    if mode == "project":
        out_specs.append(pl.BlockSpec((tm, w2.shape[1]), lambda i: (i, 0)))
        out_shape.append(jax.ShapeDtypeStruct((m, w2.shape[1]), F32))
    outs = pl.pallas_call(
        functools.partial(_out_proj_kernel, mode=mode),
        grid=(m // tm,),
        in_specs=[
            pl.BlockSpec((tm, k), lambda i: (i, 0)),
            pl.BlockSpec((None, k, d), lambda i: (layer, 0, 0), pipeline_mode=pl.Buffered(1)),
            pl.BlockSpec((tm, d), lambda i: (i, 0)),
            pl.BlockSpec((1, d), lambda i: (0, 0)),
            pl.BlockSpec(w2.shape, lambda i: (0, 0), pipeline_mode=pl.Buffered(1)),
        ],
        out_specs=out_specs,
        out_shape=out_shape,
        scratch_shapes=_weight_scratch(w),
        compiler_params=pltpu.CompilerParams(
            dimension_semantics=("arbitrary",), vmem_limit_bytes=VMEM_LIMIT),
        name="out_proj",
    )(a2d, w, x2d, g.reshape(1, d).astype(F32), w2)
    return outs if mode == "project" else outs[0]


SUBLANES = 8


def _decay_scores(q, k, f, lvl_ref, chunk):
    nt = chunk // SUBLANES
    tiles = lambda x: [x[SUBLANES * i:SUBLANES * (i + 1)] for i in range(nt)]
    cat = lambda xs: jnp.concatenate(xs, axis=0)
    row = lax.broadcasted_iota(jnp.int32, (SUBLANES, HEAD_A), 0)
    qt, kt = tiles(q), tiles(k)
    et, tt = tiles(f), tiles(f)
    gt = [None] * nt
    att = [[None] * (chunk // LANES) for _ in range(nt)]

    def take(prod, level, i, lo, hi, p=None):
        r = slice(SUBLANES * i, SUBLANES * (i + 1))
        rp = r if p is None else slice(SUBLANES * p, SUBLANES * (p + 1))
        whole = p is not None and lo % LANES == 0 and hi % LANES == 0
        for j in range(lo // LANES, (hi - 1) // LANES + 1):
            c = slice(LANES * j, LANES * (j + 1))
            if whole:
                assert att[i][j] is None
                att[i][j] = prod[rp, c]
                continue
            keep = 0.0 if att[i][j] is None else att[i][j]
            att[i][j] = jnp.where(lvl_ref[r, c] == level, prod[rp, c], keep)

    k16 = k.astype(BF16)
    prod = _dot_nt(q.astype(BF16), k16)
    for i in range(nt):
        take(prod, 0, i, SUBLANES * i, SUBLANES * (i + 1))
    level = 1
    for m in (1, 2, 4):
        a = cat([qt[i] * et[i] for i in range(nt)]).astype(BF16)
        b = k16 if m == 1 else cat([kt[i] * gt[i] for i in range(nt)]).astype(BF16)
        prod = _dot_nt(a, b)
        second = (row & m) != 0
        for i in range(nt):
            take(prod, level, i, SUBLANES * i, SUBLANES * (i + 1))
            x = jnp.where(second, pltpu.roll(tt[i], m, 0), 1.0)
            y = jnp.where(second, 1.0, pltpu.roll(tt[i], SUBLANES - m, 0))
            et[i] = et[i] * x
            gt[i] = y if gt[i] is None else gt[i] * y
            tt[i] = tt[i] * (x * y)
        level += 1
    mt = 1
    while mt < nt:
        is_second = lambda i: (i // mt) % 2 == 1
        second_tiles = [i for i in range(nt) if is_second(i)]
        a = cat([qt[i] * et[i] for i in second_tiles]).astype(BF16)
        b = cat([kt[i] if is_second(i) else kt[i] * gt[i] for i in range(nt)]).astype(BF16)
        prod = _dot_nt(a, b)
        new_t = list(tt)
        for j in range(0, nt, 2 * mt):
            t_first, t_second = tt[j], tt[j + mt]
            both = t_first * t_second
            for i in range(j, j + mt):
                gt[i] = gt[i] * t_second
                new_t[i] = both
            for i in range(j + mt, j + 2 * mt):
                take(prod, level, i, SUBLANES * j, SUBLANES * (j + mt), second_tiles.index(i))
                et[i] = et[i] * t_first
                new_t[i] = both
        tt = new_t
        mt *= 2
        level += 1
    zero = jnp.zeros((SUBLANES, LANES), F32)
    att = cat([jnp.concatenate([zero if p is None else p for p in pieces], axis=1) for pieces in att])
    return att, cat(et), cat(gt), tt[0][0:1, :]


def _hgrn_prompt_kernel(q_ref, fz_ref, v_ref, gs_ref, lbp_ref, gon_ref, lvl_ref, *rest,
                        chunk, n_chunks, heads, slab):
    o_ref, s_ref, st_scr, o_scr = rest[-4:]
    tb = pl.program_id(2)

    @pl.when(tb == 0)
    def _():
        st_scr[...] = jnp.zeros_like(st_scr)

    gon = gon_ref[...]
    o_scr[...] = jnp.zeros_like(o_scr)

    def finish(c):
        rows = pl.ds(pl.multiple_of(c * chunk, chunk), chunk)
        for h in range(heads):
            cols = slice(h * HEAD_A, (h + 1) * HEAD_A)
            y = _rms(o_scr[h], gon) * gs_ref[0, rows, cols].astype(F32)
            o_ref[0, rows, cols] = y.astype(o_ref.dtype)

    def body(c, carry):
        finish(jnp.maximum(c - 1, 0))
        rows = pl.ds(pl.multiple_of(c * chunk, chunk), chunk)
        for h in range(heads):
            cols = slice(h * HEAD_A, (h + 1) * HEAD_A)
            q = q_ref[0, rows, cols].astype(F32)
            v = v_ref[0, rows, cols]
            f, k = _gates(fz_ref[0, rows, cols], lbp_ref[0:1, cols], lbp_ref[1:2, cols])
            att, e, g, tot = _decay_scores(q, k, f, lvl_ref, chunk)
            st = st_scr[h]
            o_scr[h] = _dot_nt((q * e).astype(BF16), st.astype(BF16)) + _dot(att.astype(BF16), v)
            st_scr[h] = tot * st + _dot_tn(v, (k * g).astype(BF16))
        return carry

    lax.fori_loop(0, n_chunks, body, 0)
    finish(n_chunks - 1)

    @pl.when(tb == pl.num_programs(2) - 1)
    def _():
        s_ref[...] = jnp.zeros_like(s_ref)
        for h in range(heads):
            s_ref[slab, 0, h] = st_scr[h].T


def _level_table(chunk):
    t = lax.broadcasted_iota(jnp.int32, (chunk, chunk), 0)
    s = lax.broadcasted_iota(jnp.int32, (chunk, chunk), 1)
    x = t ^ s
    lvl = jnp.zeros((chunk, chunk), jnp.int32)
    m, level = 1, 1
    while m < chunk:
        lvl = jnp.where((x >= m) & (x < 2 * m), level, lvl)
        m *= 2
        level += 1
    return jnp.where(t >= s, lvl, -1)


def _hgrn_prompt(q, fz, v, gs, lbp, gon, states, layer, n_layers, tblk, chunk, heads):
    b, l, di = q.shape
    n_heads = di // HEAD_A
    wd = heads * HEAD_A
    col = lambda i, h, t: (i, t, h)
    in_specs = [
        pl.BlockSpec((1, tblk, wd), col),
        pl.BlockSpec((1, tblk, wd), col),
        pl.BlockSpec((1, tblk, wd), col),
        pl.BlockSpec((1, tblk, wd), col),
        pl.BlockSpec((2, wd), lambda i, h, t: (0, h)),
        pl.BlockSpec((1, HEAD_A), lambda i, h, t: (0, 0)),
        pl.BlockSpec((chunk, chunk), lambda i, h, t: (0, 0)),
    ]
    args = [q, fz, v, gs, lbp, gon.reshape(1, HEAD_A).astype(F32), _level_table(chunk)]
    aliases = {}
    if states is not None:
        in_specs.append(pl.BlockSpec(memory_space=pl.ANY))
        args.append(states)
        aliases = {len(args) - 1: 1}
    return pl.pallas_call(
        functools.partial(_hgrn_prompt_kernel, chunk=chunk, n_chunks=tblk // chunk, heads=heads,
                          slab=layer if states is None else 0),
        grid=(b, n_heads // heads, l // tblk),
        in_specs=in_specs,
        out_specs=[
            pl.BlockSpec((1, tblk, wd), col),
            (pl.BlockSpec((n_layers, 1, heads, HEAD_A, HEAD_A), lambda i, h, t: (0, i, h, 0, 0))
             if states is None else
             pl.BlockSpec((1, 1, heads, HEAD_A, HEAD_A), lambda i, h, t: (layer, i, h, 0, 0))),
        ],
        out_shape=[
            jax.ShapeDtypeStruct((b, l, di), BF16),
            jax.ShapeDtypeStruct((n_layers, b, n_heads, HEAD_A, HEAD_A), F32),
        ],
        scratch_shapes=[pltpu.VMEM((heads, HEAD_A, HEAD_A), F32), pltpu.VMEM((heads, chunk, HEAD_A), F32)],
        input_output_aliases=aliases,
        compiler_params=pltpu.CompilerParams(
            dimension_semantics=("arbitrary", "arbitrary", "arbitrary"),
            vmem_limit_bytes=VMEM_LIMIT),
        name="hgrn_prompt",
    )(*args)


def _hgrn_sample_kernel(q_ref, fz_ref, v_ref, gs_ref, s0_ref, lbp_ref, gon_ref, *rest, seqs, steps, slab,
                        fill):
    o_ref, s_ref = rest[-2:]
    gon = gon_ref[...]
    row = lax.broadcasted_iota(jnp.int32, (seqs, steps, HEAD_A), 1)
    prow = lax.broadcasted_iota(jnp.int32, (HEAD_A, HEAD_A), 0)
    pad = jnp.zeros((HEAD_A - steps, HEAD_A), F32)
    roll = lambda x, n: pltpu.roll(x, n, 1)

    q = q_ref[...]
    v = v_ref[...]
    f, k = _gates(fz_ref[...], lbp_ref[0:1, :], lbp_ref[1:2, :])
    o = jnp.sum(q * k, axis=-1, keepdims=True) * v
    w = f
    for j in range(1, steps):
        p = jnp.where(row >= j, q * roll(k, j) * w, 0.0)
        o = o + jnp.sum(p, axis=-1, keepdims=True) * roll(v, j)
        w = w * roll(f, j)
    e = f
    g = jnp.where(row < steps - 1, roll(f, steps - 1), 1.0)
    sh = 1
    while sh < steps:
        e = e * jnp.where(row >= sh, roll(e, sh), 1.0)
        g = g * jnp.where(row < steps - sh, roll(g, steps - sh), 1.0)
        sh *= 2
    qe = (q * e).astype(BF16)
    kg = k * g
    if fill:
        s_ref[...] = jnp.zeros_like(s_ref)
    outs = []
    for i in range(seqs):
        s0 = s0_ref[0, i, 0]
        outs.append(o[i] + _dot(qe[i], s0.astype(BF16)))
        z = jnp.where(prow == steps, e[i, steps - 1:steps, :], jnp.concatenate([kg[i], pad], 0))
        zt = z.T
        vp = jnp.concatenate([v[i], pad], 0)
        s_ref[slab, i, 0] = zt[:, steps:steps + 1] * s0 + _dot(zt.astype(BF16), vp.astype(BF16))
    o_ref[...] = (_rms(jnp.stack(outs), gon) * gs_ref[...]).astype(o_ref.dtype)


def _hgrn_sample(q, fz, v, gs, state_in, lbp, gon, states, layer, seqs):
    b, steps, di = q.shape
    heads = di // HEAD_A
    col = lambda i, h: (i, 0, h)
    st = lambda i, h: (layer, i, h, 0, 0)
    in_specs = [
        pl.BlockSpec((seqs, steps, HEAD_A), col),
        pl.BlockSpec((seqs, steps, HEAD_A), col),
        pl.BlockSpec((seqs, steps, HEAD_A), col),
        pl.BlockSpec((seqs, steps, HEAD_A), col),
        pl.BlockSpec((1, seqs, 1, HEAD_A, HEAD_A), st),
        pl.BlockSpec((2, HEAD_A), lambda i, h: (0, h)),
        pl.BlockSpec((1, HEAD_A), lambda i, h: (0, 0)),
    ]
    args = [q, fz, v, gs, state_in, lbp, gon.reshape(1, HEAD_A).astype(F32)]
    aliases = {}
    if states is not None:
        in_specs.append(pl.BlockSpec(memory_space=pl.ANY))
        args.append(states)
        aliases = {len(args) - 1: 1}
    return pl.pallas_call(
        functools.partial(_hgrn_sample_kernel, seqs=seqs, steps=steps, slab=layer if states is None else 0,
                          fill=states is None),
        grid=(b // seqs, heads),
        in_specs=in_specs,
        out_specs=[
            pl.BlockSpec((seqs, steps, HEAD_A), col),
            (pl.BlockSpec((state_in.shape[0], seqs, 1, HEAD_A, HEAD_A), lambda i, h: (0, i, h, 0, 0))
             if states is None else pl.BlockSpec((1, seqs, 1, HEAD_A, HEAD_A), st)),
        ],
        out_shape=[
            jax.ShapeDtypeStruct((b, steps, di), BF16),
            jax.ShapeDtypeStruct(state_in.shape, F32),
        ],
        input_output_aliases=aliases,
        compiler_params=pltpu.CompilerParams(
            dimension_semantics=("arbitrary", "arbitrary"), vmem_limit_bytes=VMEM_LIMIT),
        name="hgrn_sample",
    )(*args)


def _swa_prompt_kernel(sink_ref, q_ref, kvc_ref, kvp_ref, gs_ref, o_ref, *, kv_heads, windows):
    nb = pl.program_id(1)
    kw = kv_heads * HEAD_B
    lane_k = lax.broadcasted_iota(jnp.int32, (2 * WINDOW, LANES), 1)
    lane_q = lax.broadcasted_iota(jnp.int32, (WINDOW, LANES), 1)
    t_i = lax.broadcasted_iota(jnp.int32, (WINDOW, 2 * WINDOW), 0)
    s_i = lax.broadcasted_iota(jnp.int32, (WINDOW, 2 * WINDOW), 1)
    d = t_i + WINDOW - s_i
    band = (d >= 0) & (d <= WINDOW)
    zero = jnp.zeros((), BF16)
    pairs_per_group = GROUP_B // 2
    for wdw in range(windows):
        rows = slice(wdw * WINDOW, (wdw + 1) * WINDOW)
        prev = kvp_ref[0] if wdw == 0 else kvc_ref[0, (wdw - 1) * WINDOW:wdw * WINDOW]
        kv2 = jnp.concatenate([prev, kvc_ref[0, rows]], axis=0).astype(BF16)
        valid = band & ((s_i >= WINDOW) | (nb > 0)) if wdw == 0 else band
        for j in range(kv_heads // 2):
            kk = kv2[:, j * LANES:(j + 1) * LANES]
            vv = kv2[:, kw + j * LANES:kw + (j + 1) * LANES]
            kk_sw = pltpu.roll(kk.astype(F32), HEAD_B, 1).astype(BF16)
            vv_sw = pltpu.roll(vv.astype(F32), HEAD_B, 1).astype(BF16)
            for par in range(2):
                g = 2 * j + par
                if par == 0:
                    k2 = jnp.where(lane_k < HEAD_B, kk, kk_sw)
                    v_lo = jnp.where(lane_k < HEAD_B, vv, zero)
                    v_hi = jnp.where(lane_k >= HEAD_B, vv_sw, zero)
                else:
                    k2 = jnp.where(lane_k >= HEAD_B, kk, kk_sw)
                    v_lo = jnp.where(lane_k < HEAD_B, vv_sw, zero)
                    v_hi = jnp.where(lane_k >= HEAD_B, vv, zero)
                qs, sinks = [], []
                for half in range(2):
                    for pr in range(pairs_per_group):
                        pidx = g * pairs_per_group + pr
                        qp = q_ref[0, rows, pidx * LANES:(pidx + 1) * LANES]
                        qs.append(jnp.where((lane_q >= HEAD_B) == (half == 1), qp, zero))
                        sinks.append(jnp.full((WINDOW, LANES), sink_ref[2 * pidx + half] * LOG2E, F32))
                sink = jnp.stack(sinks)
                s = _dot_nt(jnp.concatenate(qs, axis=0), k2).reshape(GROUP_B, WINDOW, 2 * WINDOW)
                s = jnp.where(valid[None], s, -jnp.inf)
                mx = jnp.broadcast_to(jnp.max(s, axis=-1, keepdims=True), sink.shape)
                p = jnp.exp2(s - jnp.concatenate([mx, mx], axis=-1))
                den = jnp.sum(p, axis=-1, keepdims=True) + jnp.exp2(sink - mx)
                p = p.astype(BF16).reshape(GROUP_B * WINDOW, 2 * WINDOW)
                rden = (1.0 / den).reshape(2, pairs_per_group * WINDOW, LANES)
                half_rows = pairs_per_group * WINDOW
                o = (_dot(p[:half_rows], v_lo) * rden[0] + _dot(p[half_rows:], v_hi) * rden[1])
                for pr in range(pairs_per_group):
                    pidx = g * pairs_per_group + pr
                    cols = slice(pidx * LANES, (pidx + 1) * LANES)
                    gs = gs_ref[0, rows, cols].astype(F32)
                    o_ref[0, rows, cols] = (o[pr * WINDOW:(pr + 1) * WINDOW] * gs).astype(o_ref.dtype)


def _swa_prompt(q, kv, gs, sinks, windows):
    b, l, di = q.shape
    kvw = kv.shape[-1]
    kv_heads = kvw // (2 * HEAD_B)
    tq = windows * WINDOW
    return pl.pallas_call(
        functools.partial(_swa_prompt_kernel, kv_heads=kv_heads, windows=windows),
        grid=(b, l // tq),
        in_specs=[
            pl.BlockSpec(memory_space=pltpu.SMEM),
            pl.BlockSpec((1, tq, di), lambda i, n: (i, n, 0)),
            pl.BlockSpec((1, tq, kvw), lambda i, n: (i, n, 0)),
            pl.BlockSpec((1, WINDOW, kvw), lambda i, n: (i, jnp.maximum(n * windows - 1, 0), 0)),
            pl.BlockSpec((1, tq, di), lambda i, n: (i, n, 0)),
        ],
        out_specs=pl.BlockSpec((1, tq, di), lambda i, n: (i, n, 0)),
        out_shape=jax.ShapeDtypeStruct((b, l, di), BF16),
        compiler_params=pltpu.CompilerParams(
            dimension_semantics=("arbitrary", "arbitrary"), vmem_limit_bytes=VMEM_LIMIT),
        name="swa_prompt",
    )(sinks.astype(F32), q, kv, kv, gs)


def _swa_sample_kernel(sink_ref, q_ref, kvn_ref, cache_ref, gs_ref, o_ref, *, seqs, steps, kv_heads):
    kw = kv_heads * HEAD_B
    pairs = GROUP_B // 2
    rows = GROUP_B * steps
    half_rows = pairs * steps
    lane_q = lax.broadcasted_iota(jnp.int32, (steps, LANES), 1)
    lane_k = lax.broadcasted_iota(jnp.int32, (WINDOW, LANES), 1)
    t_i = lax.broadcasted_iota(jnp.int32, (rows, WINDOW), 0) % steps
    c_i = lax.broadcasted_iota(jnp.int32, (rows, WINDOW), 1)
    valid_c = c_i >= t_i
    valid_n = c_i <= t_i
    pad = jnp.zeros((WINDOW - steps, LANES), F32)
    blocks = [(i, g) for i in range(seqs) for g in range(kv_heads)]

    def kv_tiles(i, g, off):
        col = slice(off + (g // 2) * LANES, off + (g // 2 + 1) * LANES)
        xs = (cache_ref[i, :, col], jnp.concatenate([kvn_ref[i, :, col], pad], 0))
        return [(x, pltpu.roll(x, HEAD_B, 1)) for x in xs]

    scores, sinks = [], []
    for i, g in blocks:
        own_low = g % 2 == 0
        qs = []
        for half in range(2):
            for pr in range(pairs):
                pidx = g * pairs + pr
                tile = q_ref[i, :, pidx * LANES:(pidx + 1) * LANES]
                qs.append(jnp.where((lane_q >= HEAD_B) == (half == 1), tile, 0.0))
                sinks.append(jnp.full((steps, LANES), sink_ref[2 * pidx + half] * LOG2E, F32))
        q = jnp.concatenate(qs, axis=0).astype(BF16)
        kc, kn = [jnp.where((lane_k < HEAD_B) == own_low, x, sw).astype(BF16) for x, sw in kv_tiles(i, g, 0)]
        sc = jnp.where(valid_c, _dot_nt(q, kc), -jnp.inf)
        sn = jnp.where(valid_n, _dot_nt(q, kn), -jnp.inf)
        scores.append(jnp.concatenate([sc, sn], axis=1))
    s = jnp.concatenate(scores, axis=0)
    sink = jnp.concatenate(sinks, axis=0)
    mx = jnp.broadcast_to(jnp.max(s, axis=-1, keepdims=True), sink.shape)
    p = jnp.exp2(s - jnp.concatenate([mx, mx], axis=1))
    rden = 1.0 / (jnp.sum(p, axis=-1, keepdims=True) + jnp.exp2(sink - mx))
    p = p.astype(BF16)
    for n, (i, g) in enumerate(blocks):
        own_low = g % 2 == 0
        o = None
        for half in range(2):
            r = slice(n * rows + half * half_rows, n * rows + (half + 1) * half_rows)
            vc, vn = [jnp.where((lane_k >= HEAD_B) == (half == 1), x if own_low == (half == 0) else sw, 0.0)
                      .astype(BF16) for x, sw in kv_tiles(i, g, kw)]
            oh = (_dot(p[r, :WINDOW], vc) + _dot(p[r, WINDOW:], vn)) * rden[r]
            o = oh if o is None else o + oh
        for pr in range(pairs):
            cols = slice((g * pairs + pr) * LANES, (g * pairs + pr + 1) * LANES)
            o_ref[i, :, cols] = o[pr * steps:(pr + 1) * steps] * gs_ref[i, :, cols]


def _swa_sample(q, kv_new, cache, gs, sinks, seqs):
    b, steps, di = q.shape
    kvw = kv_new.shape[-1]
    kv_heads = kvw // (2 * HEAD_B)
    return pl.pallas_call(
        functools.partial(_swa_sample_kernel, seqs=seqs, steps=steps, kv_heads=kv_heads),
        grid=(b // seqs,),
        in_specs=[
            pl.BlockSpec(memory_space=pltpu.SMEM),
            pl.BlockSpec((seqs, steps, di), lambda i: (i, 0, 0)),
            pl.BlockSpec((seqs, steps, kvw), lambda i: (i, 0, 0)),
            pl.BlockSpec((seqs, WINDOW, kvw), lambda i: (i, 0, 0)),
            pl.BlockSpec((seqs, steps, di), lambda i: (i, 0, 0)),
        ],
        out_specs=pl.BlockSpec((seqs, steps, di), lambda i: (i, 0, 0)),
        out_shape=jax.ShapeDtypeStruct((b, steps, di), F32),
        compiler_params=pltpu.CompilerParams(
            dimension_semantics=("arbitrary",), vmem_limit_bytes=VMEM_LIMIT),
        name="swa_sample",
    )(sinks.astype(F32), q, kv_new, cache, gs)


def _lower_bound_params(p):
    lb = jnp.cumsum(jax.nn.softmax(p.astype(F32), axis=0), axis=0)
    lb = lb - lb[:1]
    return jnp.stack([lb, 1.0 - lb], axis=1)


def kernel(x_prompt, x_sample, state_hgrn, cache_kv_window, w_in_a, w_out_a, norm_a, onorm_a,
           lower_bounds_a, norm_kv, w_kv, w_in_b, w_out_b, norm_b, sinks_b, norm_f):
    bp, lp, dm = x_prompt.shape
    bs, ls, _ = x_sample.shape
    n_a = w_in_a.shape[0]
    n_b = w_in_b.shape[0]
    di = w_out_a.shape[1]
    kvw = w_kv.shape[1]
    kv_heads = kvw // (2 * HEAD_B)
    wb = cache_kv_window.shape[1]
    assert wb == WINDOW and lp % WINDOW == 0 and ls <= WINDOW

    lbp = _lower_bound_params(lower_bounds_a)
    hp = x_prompt.reshape(bp * lp, dm)
    hs = x_sample.reshape(bs * ls, dm)
    cache = cache_kv_window.reshape(bs, wb, kvw)

    state_prompt = state_sample = None
    kv_p = kv_s = None
    w_in_a16 = w_in_a.astype(BF16)
    for layer in range(n_a):
        project = (norm_kv, w_kv) if layer == n_a - 1 else None
        splits = dict(widths=(di,) * 4, epilogues=("silu_a", "none", "none", "silu"))
        parts = _norm_proj(hp, norm_a[layer], w_in_a16, dtypes=(BF16, F32, BF16, BF16), tm=TILES["in_proj_a_rows"],
                           layer=layer, **splits)
        og, state_prompt = _hgrn_prompt(
            *(a.reshape(bp, lp, di) for a in parts), lbp[layer], onorm_a[layer],
            state_prompt, layer, n_a, tblk=min(TILES["recurrence_tokens"], lp),
            chunk=min(TILES["recurrence_chunk"], lp), heads=TILES["recurrence_heads"])
        hp = _out_proj(og.reshape(bp * lp, di), w_out_a, hp, tm=TILES["out_proj_rows"], layer=layer, project=project)

        parts = _norm_proj(hs, norm_a[layer], w_in_a16, dtypes=(F32,) * 4, tm=TILES["sample_rows"], layer=layer, **splits)
        og, state_sample = _hgrn_sample(
            *(a.reshape(bs, ls, di) for a in parts), state_hgrn, lbp[layer], onorm_a[layer],
            state_sample, layer, seqs=min(TILES["sample_recurrence_seqs"], bs))
        hs = _out_proj(og.reshape(bs * ls, di), w_out_a, hs, tm=TILES["sample_rows"], layer=layer, project=project)
    hp, kv_p = hp
    hs, kv_s = hs
    kv_p = kv_p.reshape(bp, lp, kvw)
    kv_s = kv_s.reshape(bs, ls, kvw)

    for j in range(n_b):
        g_final = norm_f if j == n_b - 1 else None
        splits = dict(widths=(di, di), epilogues=("scale_b", "silu"))
        q, gs = _norm_proj(hp, norm_b[j], w_in_b, dtypes=(BF16, BF16), tm=TILES["in_proj_b_rows"], layer=j, **splits)
        o = _swa_prompt(q.reshape(bp, lp, di), kv_p, gs.reshape(bp, lp, di), sinks_b[j],
                        windows=TILES["attention_windows"])
        hp = _out_proj(o.reshape(bp * lp, di), w_out_b, hp, tm=TILES["out_proj_rows"], layer=j, norm=g_final)

        q, gs = _norm_proj(hs, norm_b[j], w_in_b, dtypes=(F32, F32), tm=TILES["sample_rows"], layer=j, **splits)
        o = _swa_sample(q.reshape(bs, ls, di), kv_s, cache, gs.reshape(bs, ls, di), sinks_b[j],
                        seqs=min(TILES["sample_attention_seqs"], bs))
        hs = _out_proj(o.reshape(bs * ls, di), w_out_b, hs, tm=TILES["sample_rows"], norm=g_final, layer=j)

    kv_shape = (2, kv_heads, HEAD_B)
    y_prompt = hp.reshape(bp, lp, dm)
    y_sample = hs.reshape(bs, ls, dm)
    kv_window_prompt = kv_p[:, lp - min(WINDOW, lp):].reshape(bp, min(WINDOW, lp), *kv_shape)
    kv_window_sample = jnp.concatenate([cache, kv_s], axis=1)[:, ls:].reshape(bs, wb, *kv_shape)
    return (y_prompt, y_sample, state_prompt, state_sample, kv_window_prompt, kv_window_sample)
```
